```python
import math, functools
import jax, jax.numpy as jnp
from jax import lax
import numpy as np

D_MODEL = 1024
BATCH = 16
SEQ = 2048
DEPTH = 1
DEC_BATCH = 128
DEC_SEQ = 1
PAST_LEN = 8192
PAGE_SIZE = 128

DA_HEADS = 4
DA_DH = 64
DA_VD = 2 * DA_DH
MLA_HEADS = 8
MLA_NOPE = 64
MLA_ROPE = 32
MLA_VD = 64
MLA_Q_RANK = 384
MLA_KV_RANK = 256
ROPE_THETA = 10000.0
REL_BUCKETS = 32
REL_MAX_DIST = 128
D_FF = ((8 * D_MODEL + 3 * 256 - 1) // (3 * 256)) * 256
Q_BLOCK = 128
EPS = 1e-6

DA_QK_W = DA_HEADS * 2 * DA_DH
DA_V_W = DA_HEADS * DA_VD
DA_OUT_W = DA_HEADS * DA_VD
MLA_OUT_W = MLA_HEADS * MLA_VD
IN_WIDTHS = (DA_QK_W, DA_QK_W, DA_V_W, MLA_Q_RANK, MLA_KV_RANK, MLA_ROPE, 2 * D_MODEL)
IN_W = sum(IN_WIDTHS)
IN_SPLITS = tuple(sum(IN_WIDTHS[:i + 1]) for i in range(len(IN_WIDTHS) - 1))

kernel_name = "hybrid_diffattn_mla_gated_decoder_step"


def rmsnorm(x, g):
    xf = x.astype(jnp.float32)
    y = xf * lax.rsqrt(jnp.mean(xf * xf, axis=-1, keepdims=True) + EPS)
    return (y * g.astype(jnp.float32)).astype(x.dtype)


def rope(x, pos):
    r = x.shape[-1]
    freqs = ROPE_THETA ** (-jnp.arange(0, r, 2, dtype=jnp.float32) / r)
    ang = pos.astype(jnp.float32)[:, None] * freqs[None, :]
    ang = ang.reshape((ang.shape[0],) + (1,) * (x.ndim - 3) + (r // 2,))
    cos, sin = jnp.cos(ang), jnp.sin(ang)
    xf = x.astype(jnp.float32)
    x1, x2 = xf[..., : r // 2], xf[..., r // 2:]
    return jnp.concatenate([x1 * cos - x2 * sin, x1 * sin + x2 * cos], axis=-1).astype(x.dtype)


def t5_bucket(dist):
    n = jnp.maximum(dist, 0)
    max_exact = REL_BUCKETS // 2
    nf = jnp.maximum(n, max_exact).astype(jnp.float32)
    large = max_exact + (jnp.log(nf / max_exact) / math.log(REL_MAX_DIST / max_exact)
                         * (REL_BUCKETS - max_exact)).astype(jnp.int32)
    large = jnp.minimum(large, REL_BUCKETS - 1)
    return jnp.where(n < max_exact, n, large)


def diff_attn_core(q, k, v, q_pos, k_pos, lam, rel_bias):
    kk = k.reshape(k.shape[:3] + (2, DA_DH))
    s = jnp.einsum('bqhmd,bkhmd->bhmqk', q, kk).astype(jnp.float32) * (DA_DH ** -0.5)
    dist = q_pos[:, None] - k_pos[None, :]
    bias = jnp.transpose(rel_bias[t5_bucket(dist)], (2, 0, 1)).astype(jnp.float32)
    s = jnp.where(dist >= 0, s + bias[None, :, None], -jnp.inf)
    p = jax.nn.softmax(s, axis=-1)
    a = p[:, :, 0] - lam * p[:, :, 1]
    return jnp.einsum('bhqk,bkhd->bqhd', a.astype(v.dtype), v)


def mla_core(q_nope, q_pe, ckv, kpe, q_pos, k_pos, w_uk, w_uv):
    q_lat = jnp.einsum('bqhd,rhd->bqhr', q_nope, w_uk)
    s = (jnp.einsum('bqhr,bkr->bhqk', q_lat, ckv)
         + jnp.einsum('bqhd,bkd->bhqk', q_pe, kpe)).astype(jnp.float32) * ((MLA_NOPE + MLA_ROPE) ** -0.5)
    dist = q_pos[:, None] - k_pos[None, :]
    s = jnp.where(dist >= 0, s, -jnp.inf)
    p = jax.nn.softmax(s, axis=-1)
    o_lat = jnp.einsum('bhqk,bkr->bqhr', p.astype(ckv.dtype), ckv)
    o = jnp.einsum('bqhr,rhd->bqhd', o_lat, w_uv)
    return o.reshape(o.shape[:2] + (MLA_OUT_W,))


def prompt_attend(qd, kd, vd, qn, qp, ckv, kpe, lam, rel_bias, w_uk, w_uv):
    b, s = qd.shape[:2]
    nb = s // Q_BLOCK
    pos = jnp.arange(s, dtype=jnp.int32)

    def to_blocks(t):
        return jnp.swapaxes(t.reshape((b, nb, Q_BLOCK) + t.shape[2:]), 0, 1)

    def body(args):
        qd_b, qn_b, qp_b, qpos_b = args
        oa = diff_attn_core(qd_b, kd, vd, qpos_b, pos, lam, rel_bias)
        ob = mla_core(qn_b, qp_b, ckv, kpe, qpos_b, pos, w_uk, w_uv)
        return oa, ob

    oa, ob = lax.map(body, (to_blocks(qd), to_blocks(qn), to_blocks(qp), pos.reshape(nb, Q_BLOCK)))
    oa = jnp.swapaxes(oa, 0, 1).reshape(b, s, DA_HEADS, DA_VD)
    ob = jnp.swapaxes(ob, 0, 1).reshape(b, s, MLA_OUT_W)
    return oa, ob


def sample_attend(qd, kd, vd, qn, qp, ckv, kpe, lam, rel_bias, w_uk, w_uv,
                  cache_k, cache_v, cache_ckv, cache_kpe, page_table, q_pos):
    n_new = qd.shape[1]
    past = page_table.shape[1] * cache_k.shape[1]
    k_pos = jnp.arange(past + n_new, dtype=jnp.int32)

    def gather(cache, pt, new):
        g = cache[pt]
        g = g.reshape((past,) + g.shape[2:])
        return jnp.concatenate([g, new.astype(g.dtype)], axis=0)[None]

    def body(args):
        pt, qd_i, kd_i, vd_i, qn_i, qp_i, ckv_i, kpe_i = args
        oa = diff_attn_core(qd_i[None], gather(cache_k, pt, kd_i), gather(cache_v, pt, vd_i),
                            q_pos, k_pos, lam, rel_bias)
        ob = mla_core(qn_i[None], qp_i[None], gather(cache_ckv, pt, ckv_i), gather(cache_kpe, pt, kpe_i),
                      q_pos, k_pos, w_uk, w_uv)
        return oa[0], ob[0]

    return lax.map(body, (page_table, qd, kd, vd, qn, qp, ckv, kpe))


def decoder_layer(x, c, pos, lam_init, attend, ada_w, ada_b, norm_g, w_in, da_lambda, da_subln_g,
                  mla_q_norm_g, mla_kv_norm_g, mla_w_uq, w_branch_a, w_branch_b, w_o, ffn_w_gu, ffn_w_down):
    b, s, _ = x.shape
    mod = jax.nn.silu(c) @ ada_w + ada_b
    sh1, sc1, gt1, sh2, sc2, gt2 = jnp.split(mod[:, None, :], 6, axis=-1)
    h = rmsnorm(x, norm_g[0]) * (1 + sc1) + sh1
    proj = h @ w_in
    qd, kd, vd, cq, ckv, kpe, gts = jnp.split(proj, IN_SPLITS, axis=-1)
    qd = qd.reshape(b, s, DA_HEADS, 2, DA_DH)
    kd = kd.reshape(b, s, DA_HEADS, 2 * DA_DH)
    vd = vd.reshape(b, s, DA_HEADS, DA_VD)
    cq = rmsnorm(cq, mla_q_norm_g)
    q = jnp.einsum('bsr,rhd->bshd', cq, mla_w_uq)
    qn = q[..., :MLA_NOPE]
    qp = rope(q[..., MLA_NOPE:], pos)
    ckv = rmsnorm(ckv, mla_kv_norm_g)
    kpe = rope(kpe, pos)
    lf = da_lambda.astype(jnp.float32)
    lam = jnp.exp(jnp.sum(lf[0] * lf[1])) - jnp.exp(jnp.sum(lf[2] * lf[3])) + lam_init
    oa, ob = attend(qd, kd, vd, qn, qp, ckv, kpe, lam)
    oa = (rmsnorm(oa, da_subln_g) * (1.0 - lam_init)).reshape(b, s, DA_OUT_W)
    ga, gb = jnp.split(jax.nn.sigmoid(gts), 2, axis=-1)
    merged = ga * (oa @ w_branch_a) + gb * (ob @ w_branch_b)
    x = x + gt1 * rmsnorm(merged @ w_o, norm_g[1])
    h2 = rmsnorm(x, norm_g[2]) * (1 + sc2) + sh2
    gg, uu = jnp.split(h2 @ ffn_w_gu, 2, axis=-1)
    x = x + gt2 * rmsnorm((jax.nn.silu(gg) * uu) @ ffn_w_down, norm_g[3])
    return x, kd, vd, ckv, kpe


def setup_inputs(seed: int = 0) -> dict:
    key = jax.random.key(seed)
    ks = iter(jax.random.split(key, 32))
    n_pages = PAST_LEN // PAGE_SIZE
    n_used = DEC_BATCH * n_pages
    n_pool = n_used + max(1, n_used // 4)

    def nrm(shape, scale):
        return jax.random.normal(next(ks), shape, jnp.float32) * scale

    page_table = jax.random.permutation(next(ks), n_pool)[:n_used].reshape(DEC_BATCH, n_pages).astype(jnp.int32)
    return {
        'x_prompt': nrm((BATCH, SEQ, D_MODEL), 1.0),
        'x_sample': nrm((DEC_BATCH, DEC_SEQ, D_MODEL), 1.0),
        'cache_da_k': nrm((DEPTH, n_pool, PAGE_SIZE, DA_HEADS, 2 * DA_DH), 1.0),
        'cache_da_v': nrm((DEPTH, n_pool, PAGE_SIZE, DA_HEADS, DA_VD), 1.0),
        'cache_mla_ckv': nrm((DEPTH, n_pool, PAGE_SIZE, MLA_KV_RANK), 1.0),
        'cache_mla_kpe': nrm((DEPTH, n_pool, PAGE_SIZE, MLA_ROPE), 1.0),
        'page_table': page_table,
        'c_prompt': nrm((BATCH, D_MODEL), 1.0),
        'c_sample': nrm((DEC_BATCH, D_MODEL), 1.0),
        'rel_bias': nrm((REL_BUCKETS, DA_HEADS), 0.5),
        'ada_w': nrm((DEPTH, D_MODEL, 6 * D_MODEL), 0.5 * D_MODEL ** -0.5),
        'ada_b': nrm((DEPTH, 6 * D_MODEL), 0.02),
        'norm_g': 1.0 + nrm((DEPTH, 4, D_MODEL), 0.1),
        'w_in': nrm((DEPTH, D_MODEL, IN_W), D_MODEL ** -0.5),
        'da_lambda': nrm((DEPTH, 4, DA_DH), 0.1),
        'da_subln_g': 1.0 + nrm((DEPTH, DA_VD), 0.1),
        'mla_q_norm_g': 1.0 + nrm((DEPTH, MLA_Q_RANK), 0.1),
        'mla_kv_norm_g': 1.0 + nrm((DEPTH, MLA_KV_RANK), 0.1),
        'mla_w_uq': nrm((DEPTH, MLA_Q_RANK, MLA_HEADS, MLA_NOPE + MLA_ROPE), MLA_Q_RANK ** -0.5),
        'mla_w_uk': nrm((DEPTH, MLA_KV_RANK, MLA_HEADS, MLA_NOPE), MLA_KV_RANK ** -0.5),
        'mla_w_uv': nrm((DEPTH, MLA_KV_RANK, MLA_HEADS, MLA_VD), MLA_KV_RANK ** -0.5),
        'w_branch_a': nrm((DEPTH, DA_OUT_W, D_MODEL), DA_OUT_W ** -0.5),
        'w_branch_b': nrm((DEPTH, MLA_OUT_W, D_MODEL), MLA_OUT_W ** -0.5),
        'w_o': nrm((DEPTH, D_MODEL, D_MODEL), D_MODEL ** -0.5),
        'ffn_w_gu': nrm((DEPTH, D_MODEL, 2 * D_FF), D_MODEL ** -0.5),
        'ffn_w_down': nrm((DEPTH, D_FF, D_MODEL), D_FF ** -0.5),
    }


def reference(x_prompt, x_sample, cache_da_k, cache_da_v, cache_mla_ckv, cache_mla_kpe, page_table,
              c_prompt, c_sample, rel_bias, ada_w, ada_b, norm_g, w_in, da_lambda, da_subln_g,
              mla_q_norm_g, mla_kv_norm_g, mla_w_uq, mla_w_uk, mla_w_uv, w_branch_a, w_branch_b, w_o,
              ffn_w_gu, ffn_w_down):
    y_p, y_s = x_prompt, x_sample
    pos_p = jnp.arange(x_prompt.shape[1], dtype=jnp.int32)
    past = page_table.shape[1] * cache_da_k.shape[2]
    pos_s = past + jnp.arange(x_sample.shape[1], dtype=jnp.int32)
    st_p, st_s = [], []
    for l in range(DEPTH):
        lam_init = 0.8 - 0.6 * math.exp(-0.3 * l)
        lw = (ada_w[l], ada_b[l], norm_g[l], w_in[l], da_lambda[l], da_subln_g[l], mla_q_norm_g[l],
              mla_kv_norm_g[l], mla_w_uq[l], w_branch_a[l], w_branch_b[l], w_o[l], ffn_w_gu[l], ffn_w_down[l])
        att_p = functools.partial(prompt_attend, rel_bias=rel_bias, w_uk=mla_w_uk[l], w_uv=mla_w_uv[l])
        att_s = functools.partial(sample_attend, rel_bias=rel_bias, w_uk=mla_w_uk[l], w_uv=mla_w_uv[l],
                                  cache_k=cache_da_k[l], cache_v=cache_da_v[l], cache_ckv=cache_mla_ckv[l],
                                  cache_kpe=cache_mla_kpe[l], page_table=page_table, q_pos=pos_s)
        y_p, kd_p, vd_p, ckv_p, kpe_p = decoder_layer(y_p, c_prompt, pos_p, lam_init, att_p, *lw)
        y_s, kd_s, vd_s, ckv_s, kpe_s = decoder_layer(y_s, c_sample, pos_s, lam_init, att_s, *lw)
        st_p.append((kd_p, vd_p, ckv_p, kpe_p))
        st_s.append((kd_s, vd_s, ckv_s, kpe_s))
    da_k_p = jnp.stack([t[0] for t in st_p])
    da_v_p = jnp.stack([t[1] for t in st_p])
    ckv_p_all = jnp.stack([t[2] for t in st_p])
    kpe_p_all = jnp.stack([t[3] for t in st_p])
    da_k_s = jnp.stack([t[0] for t in st_s])
    da_v_s = jnp.stack([t[1] for t in st_s])
    ckv_s_all = jnp.stack([t[2] for t in st_s])
    kpe_s_all = jnp.stack([t[3] for t in st_s])
    return (y_p, y_s, da_k_p, da_v_p, ckv_p_all, kpe_p_all, da_k_s, da_v_s, ckv_s_all, kpe_s_all)
```

```python
import functools
import math

import numpy as np
import jax
import jax.numpy as jnp
from jax import lax
from jax.experimental import pallas as pl
from jax.experimental.pallas import tpu as pltpu

F32 = jnp.float32
BF16 = jnp.bfloat16

DA_HEADS = 4
DA_DH = 64
DA_VD = 2 * DA_DH
MLA_HEADS = 8
MLA_NOPE = 64
MLA_ROPE = 32
MLA_VD = 64
ROPE_THETA = 10000.0
REL_BUCKETS = 32
REL_MAX_DIST = 128
EPS = 1e-6
DA_SCALE = DA_DH ** -0.5
MLA_SCALE = (MLA_NOPE + MLA_ROPE) ** -0.5

LANES = 128
DA_W = DA_HEADS * LANES
MLA_W = MLA_HEADS * LANES
BIAS_CUTOFF = 113
NEG = -1e30
VMEM_LIMIT = 56 * 1024 * 1024

ATTN_TILE = 256
ROW_TILE = 256
PAGES_PER_STEP = 16

NT_DIMS = (((1,), (1,)), ((), ()))
TN_DIMS = (((0,), (0,)), ((), ()))


def _rms(x, g):
    return x * lax.rsqrt(jnp.mean(x * x, axis=-1, keepdims=True) + EPS) * g


def _dot(a, b):
    return jnp.dot(a, b, preferred_element_type=F32)


def _const_spec(shape):
    nd = len(shape)
    return pl.BlockSpec(shape, lambda *_: (0,) * nd)


def _adaln_kernel(c_ref, w_ref, b_ref, o_ref):
    c = c_ref[...]
    a = (c * jax.nn.sigmoid(c)).astype(BF16)
    o_ref[...] = _dot(a, w_ref[...].astype(BF16)) + b_ref[...]


def _adaln(c_all, ada_w, ada_b):
    r, d = c_all.shape
    n = ada_w.shape[1]
    tn = 768
    return pl.pallas_call(
        _adaln_kernel,
        grid=(n // tn,),
        in_specs=[pl.BlockSpec((r, d), lambda j: (0, 0)),
                  pl.BlockSpec((d, tn), lambda j: (0, j)),
                  pl.BlockSpec((1, tn), lambda j: (0, j))],
        out_specs=pl.BlockSpec((r, tn), lambda j: (0, j)),
        out_shape=jax.ShapeDtypeStruct((r, n), F32),
        compiler_params=pltpu.CompilerParams(dimension_semantics=("arbitrary",), vmem_limit_bytes=VMEM_LIMIT),
        name="adaln",
    )(c_all, ada_w, ada_b.reshape(1, n))


_C_QD, _C_KD, _C_VD = 0, DA_W, 2 * DA_W
_C_CQ = 3 * DA_W


def _inproj_kernel(q_rank, kv_rank,
                   x_ref, sc_ref, sh_ref, g0_ref, wmain_ref, wg_ref, gq_ref, gkv_ref,
                   wuq1_ref, wuq2_ref, wukp_ref, wuvp_ref, ekpe_ref, cosq_ref, sinq_ref, tabk_ref,
                   qd_ref, kd_ref, vd_ref, kdb_ref, vdb_ref, ckv_ref, kpe_ref, qm_ref, km_ref, vm_ref, gt_ref):
    c_ckv = _C_CQ + q_rank
    c_kpe = c_ckv + kv_rank
    h = _rms(x_ref[...], g0_ref[...]) * (1.0 + sc_ref[0]) + sh_ref[0]
    hb = h.astype(BF16)

    def mm(lo, hi):
        return _dot(hb, wmain_ref[:, lo:hi])

    qd_ref[...] = (mm(_C_QD, _C_KD) * DA_SCALE).astype(BF16)
    kd = mm(_C_KD, _C_VD)
    kd_ref[...] = kd
    kdb_ref[...] = kd.astype(BF16)
    vd = mm(_C_VD, _C_CQ)
    vd_ref[...] = vd
    vdb_ref[...] = vd.astype(BF16)

    cq = _rms(mm(_C_CQ, c_ckv), gq_ref[...]).astype(BF16)
    q1 = _dot(cq, wuq1_ref[...])
    q2 = _dot(cq, wuq2_ref[...])
    cosq = cosq_ref[...]
    sinq = sinq_ref[...]
    for hh in range(MLA_HEADS):
        sl = slice(hh * LANES, (hh + 1) * LANES)
        qm_ref[:, sl] = (q1[:, sl] * cosq + q2[:, sl] * sinq).astype(BF16)

    ckv = _rms(mm(c_ckv, c_kpe), gkv_ref[...])
    ckv_ref[...] = ckv
    cb = ckv.astype(BF16)
    r = mm(c_kpe, c_kpe + LANES) * tabk_ref[...]
    kr = r + pltpu.roll(r, 3 * LANES // 4, 1)
    kpe_ref[...] = kr[:, :MLA_ROPE]
    km_ref[...] = (_dot(cb, wukp_ref[...]) + _dot(kr.astype(BF16), ekpe_ref[...])).astype(BF16)
    vm_ref[...] = _dot(cb, wuvp_ref[...]).astype(BF16)

    gt_ref[...] = jax.nn.sigmoid(_dot(hb, wg_ref[...])).astype(BF16)


def _inproj(x2d, sc, sh, w, tabs, tm, rows_per_mod_block):
    t, d = x2d.shape
    q_rank = w["gq"].shape[1]
    kv_rank = w["gkv"].shape[1]
    rmod = sc.shape[1]
    n_tab_blocks = tabs[0].shape[0] // tm
    grid = (t // tm,)
    row = lambda width: pl.BlockSpec((tm, width), lambda i: (i, 0))
    mod_spec = pl.BlockSpec((1, rmod, d), lambda i: (i // rows_per_mod_block, 0, 0))
    tab_spec = pl.BlockSpec((tm, LANES), lambda i: (i % n_tab_blocks, 0))
    weights = [w["g0"], w["w_main"], w["w_g"], w["gq"], w["gkv"], w["w_uq1"], w["w_uq2"], w["w_ukp"], w["w_uvp"],
               w["e_kpe"]]
    in_specs = ([row(d), mod_spec, mod_spec] + [_const_spec(a.shape) for a in weights] + [tab_spec] * 3)
    outs = [(DA_W, BF16), (DA_W, F32), (DA_W, F32), (DA_W, BF16), (DA_W, BF16), (kv_rank, F32), (MLA_ROPE, F32),
            (MLA_W, BF16), (MLA_W, BF16), (MLA_W, BF16), (2 * d, BF16)]
    return pl.pallas_call(
        functools.partial(_inproj_kernel, q_rank, kv_rank),
        grid=grid,
        in_specs=in_specs,
        out_specs=[row(wd) for wd, _ in outs],
        out_shape=[jax.ShapeDtypeStruct((t, wd), dt) for wd, dt in outs],
        compiler_params=pltpu.CompilerParams(dimension_semantics=("arbitrary",), vmem_limit_bytes=VMEM_LIMIT),
        name="inproj",
    )(x2d, sc, sh, w["g0"], w["w_main"], w["w_g"], w["gq"], w["gkv"], w["w_uq1"], w["w_uq2"], w["w_ukp"],
      w["w_uvp"], w["e_kpe"], *tabs)


def _t5_bucket(dist):
    n = jnp.maximum(dist, 0)
    max_exact = REL_BUCKETS // 2
    nf = jnp.maximum(n, max_exact).astype(F32)
    large = max_exact + (jnp.log(nf / max_exact) / math.log(REL_MAX_DIST / max_exact)
                         * (REL_BUCKETS - max_exact)).astype(jnp.int32)
    large = jnp.minimum(large, REL_BUCKETS - 1)
    return jnp.where(n < max_exact, n, large)


def _shifted_bias(relb_ref, bucket, head):
    last = relb_ref[REL_BUCKETS - 1, head]
    out = jnp.zeros(bucket.shape, F32)
    for b in range(REL_BUCKETS - 1):
        out = jnp.where(bucket == b, relb_ref[b, head] - last, out)
    return out


def _lambda_full(lam_ref, lam_init):
    lf = lam_ref[...]
    a = jnp.sum(lf[0:1] * lf[1:2], axis=1, keepdims=True)
    b = jnp.sum(lf[2:3] * lf[3:4], axis=1, keepdims=True)
    return jnp.exp(a) - jnp.exp(b) + lam_init


def _pattn_kernel(lam_init, relb_ref, lam_ref, subg_ref, qd_ref, qm_ref, kd_ref, vd_ref, km_ref, vm_ref,
                  oa_ref, ob_ref, bias_ref, m_ref, l_ref, acc_ref):
    t = ATTN_TILE
    qi = pl.program_id(1)

    @pl.when((pl.program_id(0) == 0) & (qi == 0))
    def _():
        key = lax.broadcasted_iota(jnp.int32, (t, t), 0)
        qry = lax.broadcasted_iota(jnp.int32, (t, t), 1)
        d0 = qry - key
        b0 = _t5_bucket(d0)
        b1 = _t5_bucket(d0 + t)
        for hh in range(DA_HEADS):
            bias_ref[hh] = jnp.where(d0 >= 0, _shifted_bias(relb_ref, b0, hh), NEG)
            bias_ref[DA_HEADS + hh] = _shifted_bias(relb_ref, b1, hh)
        bias_ref[2 * DA_HEADS] = jnp.where(d0 >= 0, 0.0, NEG)

    def two_maps(qs, k_cols, v_cols, k_ref, v_ref, near_bias, diag_bias):
        m_ref[...] = jnp.full(m_ref.shape, NEG, F32)
        l_ref[...] = jnp.zeros(l_ref.shape, F32)
        acc_ref[...] = jnp.zeros(acc_ref.shape, F32)

        def step(j, bias_idx):
            for i in range(2):
                k = k_ref[0, j, :, k_cols[i]]
                s = lax.dot_general(k, qs[i], NT_DIMS, preferred_element_type=F32)
                if bias_idx[i] is not None:
                    s = s + bias_ref[bias_idx[i]]
                m_old = m_ref[i]
                m_new = jnp.maximum(m_old, jnp.max(s, axis=0, keepdims=True))
                alpha = jnp.exp(m_old - m_new)
                p = jnp.exp(s - m_new)
                l_ref[i] = alpha * l_ref[i] + jnp.sum(p, axis=0, keepdims=True)
                pv = lax.dot_general(v_ref[0, j, :, v_cols[i]], p.astype(BF16), TN_DIMS,
                                     preferred_element_type=F32)
                acc_ref[i] = alpha * acc_ref[i] + pv
                m_ref[i] = m_new

        def far(j, carry):
            step(j, (None, None))
            return carry

        lax.fori_loop(0, qi - 1, far, 0)

        @pl.when(qi >= 1)
        def _():
            step(qi - 1, near_bias)

        step(qi, diag_bias)

    lam = _lambda_full(lam_ref, lam_init)
    lane = lax.broadcasted_iota(jnp.int32, (t, LANES), 1)
    for hh in range(DA_HEADS):
        cols = slice(hh * LANES, (hh + 1) * LANES)
        q = qd_ref[:, cols].astype(F32)
        qs = (jnp.where(lane < DA_DH, q, 0.0).astype(BF16), jnp.where(lane >= DA_DH, q, 0.0).astype(BF16))
        two_maps(qs, (cols, cols), (cols, cols), kd_ref, vd_ref,
                 (DA_HEADS + hh, DA_HEADS + hh), (hh, hh))
        ot = acc_ref[0] * (1.0 / l_ref[0]) - lam * (acc_ref[1] * (1.0 / l_ref[1]))
        o = ot.T
        oa_ref[:, cols] = (_rms(o, subg_ref[...]) * (1.0 - lam_init)).astype(BF16)

    for pair in range(MLA_HEADS // 2):
        ca = slice((2 * pair) * LANES, (2 * pair + 1) * LANES)
        cb = slice((2 * pair + 1) * LANES, (2 * pair + 2) * LANES)
        two_maps((qm_ref[:, ca], qm_ref[:, cb]), (ca, cb), (ca, cb), km_ref, vm_ref,
                 (None, None), (2 * DA_HEADS, 2 * DA_HEADS))
        ot = acc_ref[0] * (1.0 / l_ref[0]) + acc_ref[1] * (1.0 / l_ref[1])
        ob_ref[:, pair * LANES:(pair + 1) * LANES] = ot.T.astype(BF16)


def _prompt_attention(lam_init, rel_bias, da_lambda, subg, qd, qm, kdb, vdb, km, vm, batch, seq):
    t = ATTN_TILE
    nq = seq // t
    tiles = lambda a: a.reshape(batch, nq, t, a.shape[-1])
    kv_spec = lambda width: pl.BlockSpec((1, nq, t, width), lambda b, i: (b, 0, 0, 0))
    q_spec = lambda width: pl.BlockSpec((t, width), lambda b, i: (b * nq + i, 0))
    return pl.pallas_call(
        functools.partial(_pattn_kernel, lam_init),
        grid=(batch, nq),
        in_specs=[pl.BlockSpec(memory_space=pltpu.SMEM), _const_spec(da_lambda.shape), _const_spec(subg.shape),
                  q_spec(DA_W), q_spec(MLA_W), kv_spec(DA_W), kv_spec(DA_W), kv_spec(MLA_W), kv_spec(MLA_W)],
        out_specs=[q_spec(DA_W), q_spec(MLA_HEADS * MLA_VD)],
        out_shape=[jax.ShapeDtypeStruct((batch * seq, DA_W), BF16),
                   jax.ShapeDtypeStruct((batch * seq, MLA_HEADS * MLA_VD), BF16)],
        scratch_shapes=[pltpu.VMEM((2 * DA_HEADS + 1, t, t), F32), pltpu.VMEM((2, 1, t), F32),
                        pltpu.VMEM((2, 1, t), F32), pltpu.VMEM((2, LANES, t), F32)],
        compiler_params=pltpu.CompilerParams(dimension_semantics=("arbitrary", "arbitrary"),
                                             vmem_limit_bytes=VMEM_LIMIT),
        name="prompt_attention",
    )(rel_bias, da_lambda, subg, qd, qm, tiles(kdb), tiles(vdb), tiles(km), tiles(vm))


_ROWS = 16


def _sattn_kernel(lam_init, past, page, pt_ref, relb_ref, lam_ref, subg_ref, q_ref, qlat_ref, qpe_ref,
                  knew_ref, vnew_ref, cnew_ref, pnew_ref, *rest):
    npg = PAGES_PER_STEP
    kp, vp, cp, pp = (rest[i * npg:(i + 1) * npg] for i in range(4))
    oa_ref, olat_ref, mda_ref, lda_ref, ada_ref, mml_ref, lml_ref, aml_ref = rest[4 * npg:]
    g = pl.program_id(1)
    last = pl.num_programs(1) - 1

    row = lax.broadcasted_iota(jnp.int32, (_ROWS, DA_W), 0)
    lane = lax.broadcasted_iota(jnp.int32, (_ROWS, DA_W), 1)
    qrows = jnp.where(lax.shift_right_logical(lane, 6) == row,
                      jnp.broadcast_to(q_ref[0].astype(F32), (_ROWS, DA_W)), 0.0).astype(BF16)
    qlat = qlat_ref[0]
    qpe = qpe_ref[0]
    rcol = lax.broadcasted_iota(jnp.int32, (_ROWS, 1), 0)

    def head_bias(bucket):
        out = jnp.zeros(bucket.shape, F32)
        r = lax.broadcasted_iota(jnp.int32, bucket.shape, 0)
        for hh in range(DA_HEADS):
            out = jnp.where(lax.shift_right_logical(r, 1) == hh, _shifted_bias(relb_ref, bucket, hh), out)
        return out

    @pl.when(g == 0)
    def _():
        kn = knew_ref[0].astype(BF16).astype(F32)
        s0 = jnp.sum(qrows.astype(F32) * kn, axis=1, keepdims=True) + head_bias(jnp.zeros((_ROWS, 1), jnp.int32))
        mda_ref[...] = s0
        lda_ref[...] = jnp.ones((_ROWS, 1), F32)
        ada_ref[...] = jnp.broadcast_to(vnew_ref[0].astype(BF16).astype(F32), (_ROWS, DA_W))
        cn = cnew_ref[0].astype(BF16).astype(F32)
        pn = pnew_ref[0].astype(BF16).astype(F32)
        s1 = (jnp.sum(qlat.astype(F32) * cn, axis=1, keepdims=True)
              + jnp.sum(qpe.astype(F32) * pn, axis=1, keepdims=True))
        mml_ref[...] = s1
        lml_ref[...] = jnp.ones((_ROWS, 1), F32)
        aml_ref[...] = jnp.broadcast_to(cn, aml_ref.shape)

    kb = [kp[i][0].astype(BF16) for i in range(npg)]
    cb = [cp[i][0].astype(BF16) for i in range(npg)]

    s_da = [lax.dot_general(qrows, kb[i], NT_DIMS, preferred_element_type=F32) for i in range(npg)]
    pos = (g * npg + (npg - 1)) * page + lax.broadcasted_iota(jnp.int32, (_ROWS, page), 1)
    near = head_bias(_t5_bucket(past - pos))
    s_da[npg - 1] = s_da[npg - 1] + jnp.where(g == last, near, jnp.zeros_like(near))
    s_ml = [lax.dot_general(qlat, cb[i], NT_DIMS, preferred_element_type=F32)
            + lax.dot_general(qpe, pp[i][0].astype(BF16), NT_DIMS, preferred_element_type=F32)
            for i in range(npg)]

    def update(s_list, m_ref, l_ref, a_ref, values):
        m_old = m_ref[...]
        m_new = m_old
        for s in s_list:
            m_new = jnp.maximum(m_new, jnp.max(s, axis=1, keepdims=True))
        alpha = jnp.exp(m_old - m_new)
        l_new = alpha * l_ref[...]
        acc = alpha * a_ref[...]
        for s, v in zip(s_list, values):
            p = jnp.exp(s - m_new)
            l_new = l_new + jnp.sum(p, axis=1, keepdims=True)
            acc = acc + _dot(p.astype(BF16), v)
        m_ref[...] = m_new
        l_ref[...] = l_new
        a_ref[...] = acc

    update(s_da, mda_ref, lda_ref, ada_ref, [vp[i][0].astype(BF16) for i in range(npg)])
    update(s_ml, mml_ref, lml_ref, aml_ref, cb)

    @pl.when(g == last)
    def _():
        lam = _lambda_full(lam_ref, lam_init)
        accn = ada_ref[...] * (1.0 / lda_ref[...])
        own = lax.shift_right_logical(lane, 7) == lax.shift_right_logical(row, 1)
        sign = jnp.where((row & 1) == 0, jnp.ones((_ROWS, DA_W), F32), -lam)
        o = jnp.sum(jnp.where(own, accn * sign, 0.0), axis=0, keepdims=True)
        for hh in range(DA_HEADS):
            cols = slice(hh * LANES, (hh + 1) * LANES)
            oa_ref[0, :, cols] = _rms(o[:, cols], subg_ref[...]) * (1.0 - lam_init)
        olat_ref[0] = (aml_ref[...] * (1.0 / lml_ref[...]))[:MLA_HEADS]


def _sample_attention(lam_init, rel_bias, da_lambda, subg, page_table, q, qlat, qpe, knew, vnew, cnew, pnew,
                      cache_k, cache_v, cache_c, cache_p):
    n_seq, n_pages = page_table.shape
    page = cache_k.shape[1]
    kv_rank = cache_c.shape[-1]
    npg = PAGES_PER_STEP
    assert n_pages % npg == 0 and page >= BIAS_CUTOFF
    past = n_pages * page

    def page_specs(width):
        return [pl.BlockSpec((1, page, width), functools.partial(lambda i, g, pt, p: (pt[i, g * npg + p], 0, 0), p=p))
                for p in range(npg)]

    seq_spec = lambda r, width: pl.BlockSpec((1, r, width), lambda i, g, pt: (i, 0, 0))
    smem = pl.BlockSpec(memory_space=pltpu.SMEM)
    const = lambda a: pl.BlockSpec(a.shape, lambda i, g, pt: (0,) * a.ndim)
    grid_spec = pltpu.PrefetchScalarGridSpec(
        num_scalar_prefetch=1,
        grid=(n_seq, n_pages // npg),
        in_specs=([smem, const(da_lambda), const(subg), seq_spec(1, DA_W), seq_spec(_ROWS, kv_rank),
                   seq_spec(_ROWS, MLA_ROPE), seq_spec(1, DA_W), seq_spec(1, DA_W), seq_spec(1, kv_rank),
                   seq_spec(1, MLA_ROPE)]
                  + page_specs(DA_W) + page_specs(DA_W) + page_specs(kv_rank) + page_specs(MLA_ROPE)),
        out_specs=[seq_spec(1, DA_W), seq_spec(MLA_HEADS, kv_rank)],
        scratch_shapes=[pltpu.VMEM((_ROWS, 1), F32), pltpu.VMEM((_ROWS, 1), F32), pltpu.VMEM((_ROWS, DA_W), F32),
                        pltpu.VMEM((_ROWS, 1), F32), pltpu.VMEM((_ROWS, 1), F32), pltpu.VMEM((_ROWS, kv_rank), F32)],
    )
    return pl.pallas_call(
        functools.partial(_sattn_kernel, lam_init, past, page),
        grid_spec=grid_spec,
        out_shape=[jax.ShapeDtypeStruct((n_seq, 1, DA_W), F32),
                   jax.ShapeDtypeStruct((n_seq, MLA_HEADS, kv_rank), F32)],
        compiler_params=pltpu.CompilerParams(dimension_semantics=("arbitrary", "arbitrary"),
                                             vmem_limit_bytes=VMEM_LIMIT),
        name="sample_attention",
    )(page_table, rel_bias, da_lambda, subg, q, qlat, qpe, knew, vnew, cnew, pnew,
      *([cache_k] * npg), *([cache_v] * npg), *([cache_c] * npg), *([cache_p] * npg))


def _headproj_kernel(n_heads, in_w, out_w, x_ref, w_ref, o_ref):
    for hh in range(n_heads):
        x = x_ref[:, hh * in_w:(hh + 1) * in_w].astype(BF16)
        o_ref[:, hh * out_w:(hh + 1) * out_w] = _dot(x, w_ref[hh]).astype(o_ref.dtype)


def _headproj(x, w, out_dtype, name):
    n_heads, in_w, out_w = w.shape
    r = x.shape[0]
    return pl.pallas_call(
        functools.partial(_headproj_kernel, n_heads, in_w, out_w),
        out_shape=jax.ShapeDtypeStruct((r, n_heads * out_w), out_dtype),
        compiler_params=pltpu.CompilerParams(vmem_limit_bytes=VMEM_LIMIT),
        name=name,
    )(x, w)


def _post_kernel(d_ff, x_ref, oa_ref, ob_ref, gt_ref, gt1_ref, sh2_ref, sc2_ref, gt2_ref, ng_ref,
                 wa_ref, wb_ref, wo_ref, wgu_ref, wdn_ref, y_ref):
    d = x_ref.shape[1]
    x = x_ref[...]
    gates = gt_ref[...]
    merged = (gates[:, :d].astype(F32) * _dot(oa_ref[...], wa_ref[...])
              + gates[:, d:].astype(F32) * _dot(ob_ref[...], wb_ref[...]))
    x1 = x + gt1_ref[0] * _rms(_dot(merged.astype(BF16), wo_ref[...]), ng_ref[1:2])
    h2 = (_rms(x1, ng_ref[2:3]) * (1.0 + sc2_ref[0]) + sh2_ref[0]).astype(BF16)
    gg = _dot(h2, wgu_ref[:, :d_ff])
    uu = _dot(h2, wgu_ref[:, d_ff:])
    act = (gg * jax.nn.sigmoid(gg) * uu).astype(BF16)
    y_ref[...] = x1 + gt2_ref[0] * _rms(_dot(act, wdn_ref[...]), ng_ref[3:4])


def _post(x2d, oa, ob, gates, mods, w, tm, rows_per_mod_block):
    t, d = x2d.shape
    d_ff = w["w_dn"].shape[0]
    rmod = mods[0].shape[1]
    row = lambda width: pl.BlockSpec((tm, width), lambda i: (i, 0))
    mod_spec = pl.BlockSpec((1, rmod, d), lambda i: (i // rows_per_mod_block, 0, 0))
    weights = [w["ng"], w["w_a"], w["w_b"], w["w_o"], w["w_gu"], w["w_dn"]]
    return pl.pallas_call(
        functools.partial(_post_kernel, d_ff),
        grid=(t // tm,),
        in_specs=[row(d), row(oa.shape[1]), row(ob.shape[1]), row(2 * d)] + [mod_spec] * 4
                 + [_const_spec(a.shape) for a in weights],
        out_specs=row(d),
        out_shape=jax.ShapeDtypeStruct((t, d), F32),
        compiler_params=pltpu.CompilerParams(dimension_semantics=("arbitrary",), vmem_limit_bytes=VMEM_LIMIT),
        name="post",
    )(x2d, oa, ob, gates, *mods, *weights)


def _layer_weights(norm_g, w_in, mla_q_norm_g, mla_kv_norm_g, mla_w_uq, mla_w_uk, mla_w_uv, w_branch_a,
                   w_branch_b, w_o, ffn_w_gu, ffn_w_down):
    d = w_in.shape[0]
    q_rank = mla_q_norm_g.shape[0]
    kv_rank = mla_kv_norm_g.shape[0]
    half = MLA_ROPE // 2
    c0 = 3 * DA_W + q_rank + kv_rank
    w_kpe = w_in[:, c0:c0 + MLA_ROPE]
    partner = jnp.concatenate([w_kpe[:, half:], w_kpe[:, :half]], axis=1)
    w_main = jnp.concatenate([w_in[:, :c0], w_kpe, partner, jnp.zeros((d, MLA_ROPE), F32), partner], axis=1)
    w_g = w_in[:, c0 + MLA_ROPE:]

    pad = LANES - MLA_NOPE - MLA_ROPE
    zq = lambda n: jnp.zeros((q_rank, MLA_HEADS, n), F32)
    nope, r1, r2 = mla_w_uq[..., :MLA_NOPE], mla_w_uq[..., MLA_NOPE:MLA_NOPE + half], mla_w_uq[..., MLA_NOPE + half:]
    w_uq1 = jnp.concatenate([nope, r1, r2, zq(pad)], axis=-1).reshape(q_rank, MLA_W)
    w_uq2 = jnp.concatenate([zq(MLA_NOPE), r2, r1, zq(pad)], axis=-1).reshape(q_rank, MLA_W)
    zk = lambda n: jnp.zeros((kv_rank, MLA_HEADS, n), F32)
    w_ukp = jnp.concatenate([mla_w_uk, zk(LANES - MLA_NOPE)], axis=-1).reshape(kv_rank, MLA_W)
    e = np.zeros((LANES, MLA_HEADS, LANES), np.float32)
    for l in range(MLA_ROPE):
        e[l, :, MLA_NOPE + l] = 1.0
    e_kpe = jnp.asarray(e.reshape(LANES, MLA_W))
    even = jnp.concatenate([mla_w_uv, zk(LANES - MLA_VD)], axis=-1)
    odd = jnp.concatenate([zk(LANES - MLA_VD), mla_w_uv], axis=-1)
    is_even = (jnp.arange(MLA_HEADS) % 2 == 0)[None, :, None]
    w_uvp = jnp.where(is_even, even, odd).reshape(kv_rank, MLA_W)
    w_lat = jnp.concatenate([jnp.transpose(mla_w_uk, (1, 2, 0)),
                             jnp.zeros((MLA_HEADS, LANES - MLA_NOPE, kv_rank), F32)], axis=1)
    w_val = jnp.transpose(mla_w_uv, (1, 0, 2))
    bf = lambda a: a.astype(BF16)
    return dict(
        g0=norm_g[0:1], ng=norm_g, gq=mla_q_norm_g[None], gkv=mla_kv_norm_g[None],
        w_main=bf(w_main), w_g=bf(w_g), w_uq1=bf(w_uq1), w_uq2=bf(w_uq2), w_ukp=bf(w_ukp), w_uvp=bf(w_uvp),
        e_kpe=bf(e_kpe), w_lat=bf(w_lat), w_val=bf(w_val),
        w_a=bf(w_branch_a), w_b=bf(w_branch_b), w_o=bf(w_o), w_gu=bf(ffn_w_gu), w_dn=bf(ffn_w_down))


def _rope_tables(pos):
    freqs = ROPE_THETA ** (-jnp.arange(0, MLA_ROPE, 2, dtype=F32) / MLA_ROPE)
    ang = pos.astype(F32)[:, None] * freqs[None, :]
    cos, sin = jnp.cos(ang), jnp.sin(ang)
    n = pos.shape[0]
    one = jnp.ones((n, MLA_NOPE), F32)
    zero = lambda w: jnp.zeros((n, w), F32)
    pad = LANES - MLA_NOPE - MLA_ROPE
    cosq = jnp.concatenate([one, cos, cos, zero(pad)], axis=1) * MLA_SCALE
    sinq = jnp.concatenate([zero(MLA_NOPE), -sin, sin, zero(pad)], axis=1) * MLA_SCALE
    part = jnp.concatenate([-sin, sin], axis=1)
    tabk = jnp.concatenate([cos, cos, part, zero(MLA_ROPE), part], axis=1)
    return cosq, sinq, tabk


def _layer(l, x_prompt, x_sample, cache_k, cache_v, cache_c, cache_p, page_table, c_prompt, c_sample, rel_bias,
           ada_w, ada_b, da_lambda, da_subln_g, w):
    batch, seq, d = x_prompt.shape
    n_seq, dec_seq, _ = x_sample.shape
    assert dec_seq == 1
    lam_init = 0.8 - 0.6 * math.exp(-0.3 * l)
    n_pool, page = cache_k.shape[:2]
    past = page_table.shape[1] * page
    subg = da_subln_g[None]

    mod = _adaln(jnp.concatenate([c_prompt, c_sample], axis=0), ada_w, ada_b)
    mods_p = [mod[:batch, i * d:(i + 1) * d].reshape(batch, 1, d) for i in range(6)]
    mods_s = [mod[batch:, i * d:(i + 1) * d].reshape(1, n_seq, d) for i in range(6)]

    xp = x_prompt.reshape(batch * seq, d)
    tm = ROW_TILE
    (qd, kd, vd, kdb, vdb, ckv, kpe, qm, km, vm, gates) = _inproj(
        xp, mods_p[1], mods_p[0], w, _rope_tables(jnp.arange(seq, dtype=jnp.int32)), tm, seq // tm)
    oa, ob = _prompt_attention(lam_init, rel_bias, da_lambda, subg, qd, qm, kdb, vdb, km, vm, batch, seq)
    y_p = _post(xp, oa, ob, gates, [mods_p[2], mods_p[3], mods_p[4], mods_p[5]], w, tm, seq // tm)
    state_p = (kd.reshape(batch, seq, DA_HEADS, 2 * DA_DH), vd.reshape(batch, seq, DA_HEADS, DA_VD),
               ckv.reshape(batch, seq, -1), kpe.reshape(batch, seq, MLA_ROPE))

    xs = x_sample.reshape(n_seq, d)
    pos_s = jnp.full((n_seq,), past, jnp.int32)
    (qd_s, kd_s, vd_s, _, _, ckv_s, kpe_s, qm_s, _, _, gates_s) = _inproj(
        xs, mods_s[1], mods_s[0], w, _rope_tables(pos_s), n_seq, 1)
    kv_rank = ckv_s.shape[1]
    qlat = _headproj(qm_s, w["w_lat"], BF16, "latent_query").reshape(n_seq, MLA_HEADS, kv_rank)
    qpe = qm_s.reshape(n_seq, MLA_HEADS, LANES)[:, :, MLA_NOPE:MLA_NOPE + MLA_ROPE]
    pad_rows = lambda a: jnp.pad(a, ((0, 0), (0, _ROWS - a.shape[1]), (0, 0)))
    oa_s, olat = _sample_attention(
        lam_init, rel_bias, da_lambda, subg, page_table, qd_s.reshape(n_seq, 1, DA_W), pad_rows(qlat), pad_rows(qpe),
        kd_s.reshape(n_seq, 1, DA_W), vd_s.reshape(n_seq, 1, DA_W), ckv_s.reshape(n_seq, 1, kv_rank),
        kpe_s.reshape(n_seq, 1, MLA_ROPE),
        cache_k.reshape(n_pool, page, DA_W), cache_v.reshape(n_pool, page, DA_W), cache_c, cache_p)
    ob_s = _headproj(olat.reshape(n_seq, MLA_HEADS * kv_rank), w["w_val"], BF16, "value_up")
    y_s = _post(xs, oa_s.reshape(n_seq, DA_W).astype(BF16), ob_s, gates_s,
                [mods_s[2], mods_s[3], mods_s[4], mods_s[5]], w, n_seq, 1)
    state_s = (kd_s.reshape(n_seq, 1, DA_HEADS, 2 * DA_DH), vd_s.reshape(n_seq, 1, DA_HEADS, DA_VD),
               ckv_s.reshape(n_seq, 1, -1), kpe_s.reshape(n_seq, 1, MLA_ROPE))
    return y_p.reshape(batch, seq, d), y_s.reshape(n_seq, 1, d), state_p, state_s


def kernel(x_prompt, x_sample, cache_da_k, cache_da_v, cache_mla_ckv, cache_mla_kpe, page_table, c_prompt, c_sample, rel_bias, ada_w, ada_b, norm_g, w_in, da_lambda, da_subln_g, mla_q_norm_g, mla_kv_norm_g, mla_w_uq, mla_w_uk, mla_w_uv, w_branch_a, w_branch_b, w_o, ffn_w_gu, ffn_w_down):
    depth = ada_w.shape[0]
    y_p, y_s = x_prompt, x_sample
    st_p, st_s = [], []
    for l in range(depth):
        w = _layer_weights(norm_g[l], w_in[l], mla_q_norm_g[l], mla_kv_norm_g[l], mla_w_uq[l], mla_w_uk[l],
                           mla_w_uv[l], w_branch_a[l], w_branch_b[l], w_o[l], ffn_w_gu[l], ffn_w_down[l])
        y_p, y_s, sp, ss = _layer(l, y_p, y_s, cache_da_k[l], cache_da_v[l], cache_mla_ckv[l], cache_mla_kpe[l],
                                  page_table, c_prompt, c_sample, rel_bias, ada_w[l], ada_b[l], da_lambda[l],
                                  da_subln_g[l], w)
        st_p.append(sp)
        st_s.append(ss)
    stack = lambda sts, i: jnp.stack([s[i] for s in sts])
    return (y_p, y_s, stack(st_p, 0), stack(st_p, 1), stack(st_p, 2), stack(st_p, 3),
            stack(st_s, 0), stack(st_s, 1), stack(st_s, 2), stack(st_s, 3))
```

```python
import functools
import math

import numpy as np
import jax
import jax.numpy as jnp
from jax import lax
from jax.experimental import pallas as pl
from jax.experimental.pallas import tpu as pltpu

F32 = jnp.float32
BF16 = jnp.bfloat16

DA_HEADS = 4
DA_DH = 64
DA_VD = 2 * DA_DH
MLA_HEADS = 8
MLA_NOPE = 64
MLA_ROPE = 32
MLA_VD = 64
ROPE_THETA = 10000.0
REL_BUCKETS = 32
REL_MAX_DIST = 128
EPS = 1e-6
LOG2E = math.log2(math.e)
DA_SCALE = DA_DH ** -0.5 * LOG2E
MLA_SCALE = (MLA_NOPE + MLA_ROPE) ** -0.5 * LOG2E

LANES = 128
SUBLANES = 8
DA_W = DA_HEADS * LANES
MLA_W = MLA_HEADS * LANES
MLA_VW = MLA_HEADS * MLA_VD
BIAS_CUTOFF = 113
NEG = -1e30
VMEM_LIMIT = 56 * 1024 * 1024

TILE = 256
PAGES_PER_STEP = 16
N_MAPS = 2 * DA_HEADS + MLA_HEADS

NT_DIMS = (((1,), (1,)), ((), ()))


def _rms(x, g):
    return x * lax.rsqrt(jnp.mean(x * x, axis=-1, keepdims=True) + EPS) * g


def _dot(a, b):
    return jnp.dot(a, b, preferred_element_type=F32)


def _dot_nt(a, b):
    return lax.dot_general(a, b, NT_DIMS, preferred_element_type=F32)


def _const_spec(shape):
    nd = len(shape)
    return pl.BlockSpec(shape, lambda *_: (0,) * nd)


def _adaln_kernel(c_ref, w_ref, b_ref, o_ref):
    c = c_ref[...]
    a = (c * jax.nn.sigmoid(c)).astype(BF16)
    o_ref[...] = _dot(a, w_ref[...].astype(BF16)) + b_ref[...]


def _adaln(c_all, ada_w, ada_b):
    r, d = c_all.shape
    n = ada_w.shape[1]
    tn = 768
    return pl.pallas_call(
        _adaln_kernel,
        grid=(n // tn,),
        in_specs=[pl.BlockSpec((r, d), lambda j: (0, 0)),
                  pl.BlockSpec((d, tn), lambda j: (0, j)),
                  pl.BlockSpec((1, tn), lambda j: (0, j))],
        out_specs=pl.BlockSpec((r, tn), lambda j: (0, j)),
        out_shape=jax.ShapeDtypeStruct((r, n), F32),
        compiler_params=pltpu.CompilerParams(dimension_semantics=("arbitrary",), vmem_limit_bytes=VMEM_LIMIT),
        name="adaln",
    )(c_all, ada_w, ada_b.reshape(1, n))


_C_QD, _C_KD, _C_VD = 0, DA_W, 2 * DA_W
_C_CQ = 3 * DA_W


def _inproj_kernel(q_rank, kv_rank,
                   x_ref, sc_ref, sh_ref, g0_ref, wmain_ref, wg_ref, gq_ref, gkv_ref,
                   wuq1_ref, wuq2_ref, wukp_ref, wuvt_ref, ekpe_ref, cosq_ref, sinq_ref, tabk_ref,
                   qd_ref, kd_ref, vd_ref, kdb_ref, vdt_ref, ckv_ref, kpe_ref, qm_ref, km_ref, vmt_ref, gt_ref):
    c_ckv = _C_CQ + q_rank
    c_kpe = c_ckv + kv_rank
    h = _rms(x_ref[...], g0_ref[...]) * (1.0 + sc_ref[0]) + sh_ref[0]
    hb = h.astype(BF16)

    def mm(lo, hi):
        return _dot(hb, wmain_ref[:, lo:hi])

    qd_ref[...] = (mm(_C_QD, _C_KD) * DA_SCALE).astype(BF16)
    kd = mm(_C_KD, _C_VD)
    vd = mm(_C_VD, _C_CQ)
    for hh in range(DA_HEADS):
        cols = slice(hh * LANES, (hh + 1) * LANES)
        kd_ref[:, hh, :] = kd[:, cols]
        vd_ref[:, hh, :] = vd[:, cols]
    kdb_ref[...] = kd.astype(BF16)
    vdt_ref[0] = vd.T.astype(BF16)

    cq = _rms(mm(_C_CQ, c_ckv), gq_ref[...]).astype(BF16)
    q1 = _dot(cq, wuq1_ref[...])
    q2 = _dot(cq, wuq2_ref[...])
    cosq = cosq_ref[...]
    sinq = sinq_ref[...]
    for hh in range(MLA_HEADS):
        sl = slice(hh * LANES, (hh + 1) * LANES)
        qm_ref[:, sl] = (q1[:, sl] * cosq + q2[:, sl] * sinq).astype(BF16)

    ckv = _rms(mm(c_ckv, c_kpe), gkv_ref[...])
    ckv_ref[...] = ckv
    cb = ckv.astype(BF16)
    r = mm(c_kpe, c_kpe + LANES) * tabk_ref[...]
    kr = r + pltpu.roll(r, 3 * LANES // 4, 1)
    kpe_ref[...] = kr[:, :MLA_ROPE]
    km_ref[...] = (_dot(cb, wukp_ref[...]) + _dot(kr.astype(BF16), ekpe_ref[...])).astype(BF16)
    vmt_ref[0] = _dot_nt(wuvt_ref[...], cb).astype(BF16)

    gt_ref[...] = jax.nn.sigmoid(_dot(hb, wg_ref[...])).astype(BF16)


def _inproj(x2d, sc, sh, w, tabs, tm, rows_per_mod_block):
    t, d = x2d.shape
    q_rank = w["gq"].shape[1]
    kv_rank = w["gkv"].shape[1]
    rmod = sc.shape[1]
    n_tab_blocks = tabs[0].shape[0] // tm
    n_tiles = t // tm
    row = lambda width: pl.BlockSpec((tm, width), lambda i: (i, 0))
    heads = pl.BlockSpec((tm, DA_HEADS, LANES), lambda i: (i, 0, 0))
    transposed = lambda width: pl.BlockSpec((1, width, tm), lambda i: (i, 0, 0))
    mod_spec = pl.BlockSpec((1, rmod, d), lambda i: (i // rows_per_mod_block, 0, 0))
    tab_spec = pl.BlockSpec((tm, LANES), lambda i: (i % n_tab_blocks, 0))
    weights = [w["g0"], w["w_main"], w["w_g"], w["gq"], w["gkv"], w["w_uq1"], w["w_uq2"], w["w_ukp"], w["w_uvt"],
               w["e_kpe"]]
    in_specs = ([row(d), mod_spec, mod_spec] + [_const_spec(a.shape) for a in weights] + [tab_spec] * 3)
    sds = jax.ShapeDtypeStruct
    outs = [
        (row(DA_W), sds((t, DA_W), BF16)),
        (heads, sds((t, DA_HEADS, LANES), F32)),
        (heads, sds((t, DA_HEADS, LANES), F32)),
        (row(DA_W), sds((t, DA_W), BF16)),
        (transposed(DA_W), sds((n_tiles, DA_W, tm), BF16)),
        (row(kv_rank), sds((t, kv_rank), F32)),
        (row(MLA_ROPE), sds((t, MLA_ROPE), F32)),
        (row(MLA_W), sds((t, MLA_W), BF16)),
        (row(MLA_W), sds((t, MLA_W), BF16)),
        (transposed(MLA_VW), sds((n_tiles, MLA_VW, tm), BF16)),
        (row(2 * d), sds((t, 2 * d), BF16)),
    ]
    return pl.pallas_call(
        functools.partial(_inproj_kernel, q_rank, kv_rank),
        grid=(n_tiles,),
        in_specs=in_specs,
        out_specs=[o[0] for o in outs],
        out_shape=[o[1] for o in outs],
        compiler_params=pltpu.CompilerParams(dimension_semantics=("arbitrary",), vmem_limit_bytes=VMEM_LIMIT),
        name="inproj",
    )(x2d, sc, sh, *weights, *tabs)


def _t5_bucket(dist):
    n = jnp.maximum(dist, 0)
    max_exact = REL_BUCKETS // 2
    nf = jnp.maximum(n, max_exact).astype(F32)
    large = max_exact + (jnp.log(nf / max_exact) / math.log(REL_MAX_DIST / max_exact)
                         * (REL_BUCKETS - max_exact)).astype(jnp.int32)
    large = jnp.minimum(large, REL_BUCKETS - 1)
    return jnp.where(n < max_exact, n, large)


def _shifted_bias(relb_ref, bucket, head):
    last = relb_ref[REL_BUCKETS - 1, head]
    out = jnp.zeros(bucket.shape, F32)
    for b in range(REL_BUCKETS - 1):
        out = jnp.where(bucket == b, (relb_ref[b, head] - last) * LOG2E, out)
    return out


def _lambda_full(lam_ref, lam_init):
    lf = lam_ref[...]
    a = jnp.sum(lf[0:1] * lf[1:2], axis=1, keepdims=True)
    b = jnp.sum(lf[2:3] * lf[3:4], axis=1, keepdims=True)
    return jnp.exp(a) - jnp.exp(b) + lam_init


def _pattn_kernel(lam_init, relb_ref, lam_ref, subg_ref, qd_ref, qm_ref, kd_ref, vdt_ref, km_ref, vmt_ref,
                  oa_ref, ob_ref, bias_ref, qh_ref, m_ref, l_ref, acc_da_ref, acc_ml_ref):
    t = TILE
    qi = pl.program_id(1)
    n_da = 2 * DA_HEADS

    @pl.when((pl.program_id(0) == 0) & (qi == 0))
    def _():
        key = lax.broadcasted_iota(jnp.int32, (t, t), 0)
        qry = lax.broadcasted_iota(jnp.int32, (t, t), 1)
        d0 = qry - key
        b0 = _t5_bucket(d0)
        b1 = _t5_bucket(d0 + t)
        for hh in range(DA_HEADS):
            bias_ref[hh] = jnp.where(d0 >= 0, _shifted_bias(relb_ref, b0, hh), NEG)
            bias_ref[DA_HEADS + hh] = _shifted_bias(relb_ref, b1, hh)
        bias_ref[2 * DA_HEADS] = jnp.where(d0 >= 0, 0.0, NEG)

    lane = lax.broadcasted_iota(jnp.int32, (t, LANES), 1)
    for hh in range(DA_HEADS):
        q = qd_ref[:, hh * LANES:(hh + 1) * LANES].astype(F32)
        qh_ref[2 * hh] = jnp.where(lane < DA_DH, q, 0.0).astype(BF16)
        qh_ref[2 * hh + 1] = jnp.where(lane >= DA_DH, q, 0.0).astype(BF16)

    m_ref[...] = jnp.full(m_ref.shape, NEG, F32)
    l_ref[...] = jnp.zeros(l_ref.shape, F32)
    acc_da_ref[...] = jnp.zeros(acc_da_ref.shape, F32)
    acc_ml_ref[...] = jnp.zeros(acc_ml_ref.shape, F32)

    def step(j, kind):
        scores = []
        for i in range(N_MAPS):
            if i < n_da:
                hh = i // 2
                k = kd_ref[0, j, :, hh * LANES:(hh + 1) * LANES]
                q = qh_ref[i]
                bias = {"far": None, "near": DA_HEADS + hh, "diag": hh}[kind]
            else:
                hh = i - n_da
                k = km_ref[0, j, :, hh * LANES:(hh + 1) * LANES]
                q = qm_ref[:, hh * LANES:(hh + 1) * LANES]
                bias = {"far": None, "near": None, "diag": 2 * DA_HEADS}[kind]
            s = _dot_nt(k, q)
            if bias is not None:
                s = s + bias_ref[bias]
            scores.append(s)
        probs = []
        for i in range(N_MAPS):
            s = scores[i]
            m_old = m_ref[i]
            m_new = jnp.maximum(m_old, jnp.max(s, axis=0, keepdims=True))
            alpha = jnp.exp2(m_old - m_new)
            p = jnp.exp2(s - m_new)
            l_ref[i] = alpha * l_ref[i] + jnp.sum(p, axis=0, keepdims=True)
            m_ref[i] = m_new
            probs.append((alpha, p.astype(BF16)))
        for i in range(N_MAPS):
            alpha, p = probs[i]
            if i < n_da:
                hh = i // 2
                vt = vdt_ref[j, hh * LANES:(hh + 1) * LANES, :]
                acc_ref, ai = acc_da_ref, i
            else:
                hh = i - n_da
                vt = vmt_ref[j, hh * MLA_VD:(hh + 1) * MLA_VD, :]
                acc_ref, ai = acc_ml_ref, hh
            acc_ref[ai] = alpha * acc_ref[ai] + _dot(vt, p)

    def far(j, carry):
        step(j, "far")
        return carry

    lax.fori_loop(0, qi - 1, far, 0)

    @pl.when(qi >= 1)
    def _():
        step(qi - 1, "near")

    step(qi, "diag")

    lam = _lambda_full(lam_ref, lam_init)
    for hh in range(DA_HEADS):
        ot = (acc_da_ref[2 * hh] * (1.0 / l_ref[2 * hh])
              - lam * (acc_da_ref[2 * hh + 1] * (1.0 / l_ref[2 * hh + 1])))
        oa_ref[:, hh * LANES:(hh + 1) * LANES] = (_rms(ot.T, subg_ref[...]) * (1.0 - lam_init)).astype(BF16)
    for pair in range(MLA_HEADS // 2):
        ha, hb = 2 * pair, 2 * pair + 1
        ot = jnp.concatenate([acc_ml_ref[ha] * (1.0 / l_ref[n_da + ha]),
                              acc_ml_ref[hb] * (1.0 / l_ref[n_da + hb])], axis=0)
        ob_ref[:, pair * LANES:(pair + 1) * LANES] = ot.T.astype(BF16)


def _prompt_attention(lam_init, rel_bias, da_lambda, subg, qd, qm, kdb, vdt, km, vmt, batch, seq):
    t = TILE
    nq = seq // t
    tiles = lambda a: a.reshape(batch, nq, t, a.shape[-1])
    kv_spec = lambda width: pl.BlockSpec((1, nq, t, width), lambda b, i: (b, 0, 0, 0))
    vt_spec = lambda width: pl.BlockSpec((nq, width, t), lambda b, i: (b, 0, 0))
    q_spec = lambda width: pl.BlockSpec((t, width), lambda b, i: (b * nq + i, 0))
    return pl.pallas_call(
        functools.partial(_pattn_kernel, lam_init),
        grid=(batch, nq),
        in_specs=[pl.BlockSpec(memory_space=pltpu.SMEM), _const_spec(da_lambda.shape), _const_spec(subg.shape),
                  q_spec(DA_W), q_spec(MLA_W), kv_spec(DA_W), vt_spec(DA_W), kv_spec(MLA_W), vt_spec(MLA_VW)],
        out_specs=[q_spec(DA_W), q_spec(MLA_VW)],
        out_shape=[jax.ShapeDtypeStruct((batch * seq, DA_W), BF16),
                   jax.ShapeDtypeStruct((batch * seq, MLA_VW), BF16)],
        scratch_shapes=[pltpu.VMEM((2 * DA_HEADS + 1, t, t), F32), pltpu.VMEM((2 * DA_HEADS, t, LANES), BF16),
                        pltpu.VMEM((N_MAPS, 1, t), F32), pltpu.VMEM((N_MAPS, 1, t), F32),
                        pltpu.VMEM((2 * DA_HEADS, LANES, t), F32), pltpu.VMEM((MLA_HEADS, MLA_VD, t), F32)],
        compiler_params=pltpu.CompilerParams(dimension_semantics=("arbitrary", "arbitrary"),
                                             vmem_limit_bytes=VMEM_LIMIT),
        name="prompt_attention",
    )(rel_bias, da_lambda, subg, qd, qm, tiles(kdb), vdt, tiles(km), vmt)


_ROWS = 16


def _sattn_kernel(lam_init, past, page, pt_ref, relb_ref, lam_ref, subg_ref, q_ref, qlat_ref, qpe_ref,
                  knew_ref, vnew_ref, cnew_ref, pnew_ref, *rest):
    npg = PAGES_PER_STEP
    kp, vp, cp, pp = (rest[i * npg:(i + 1) * npg] for i in range(4))
    oa_ref, olat_ref, mda_ref, lda_ref, ada_ref, mml_ref, lml_ref, aml_ref = rest[4 * npg:]
    g = pl.program_id(1)
    last = pl.num_programs(1) - 1
    rows_pp = page * DA_HEADS

    lane8 = lax.broadcasted_iota(jnp.int32, (SUBLANES, LANES), 1)
    q8 = q_ref[0].astype(F32)
    qrows = jnp.concatenate([jnp.where(lane8 < DA_DH, q8, 0.0), jnp.where(lane8 >= DA_DH, q8, 0.0)],
                            axis=0).astype(BF16)
    qlat = qlat_ref[0]
    qpe = qpe_ref[0]
    row = lax.broadcasted_iota(jnp.int32, (_ROWS, rows_pp), 0)
    col = lax.broadcasted_iota(jnp.int32, (_ROWS, rows_pp), 1)
    own = (col & (DA_HEADS - 1)) == (row & (SUBLANES - 1))

    def head_bias(bucket):
        out = jnp.zeros(bucket.shape, F32)
        r = lax.broadcasted_iota(jnp.int32, bucket.shape, 0) & (SUBLANES - 1)
        for hh in range(DA_HEADS):
            out = jnp.where(r == hh, _shifted_bias(relb_ref, bucket, hh), out)
        return out

    @pl.when(g == 0)
    def _():
        kn = knew_ref[0].astype(BF16).astype(F32)
        kn = jnp.concatenate([kn, kn], axis=0)
        s0 = (jnp.sum(qrows.astype(F32) * kn, axis=1, keepdims=True)
              + head_bias(jnp.zeros((_ROWS, 1), jnp.int32)))
        mda_ref[...] = s0
        lda_ref[...] = jnp.ones((_ROWS, 1), F32)
        vn = vnew_ref[0].astype(BF16).astype(F32)
        ada_ref[...] = jnp.concatenate([vn, vn], axis=0)
        cn = cnew_ref[0].astype(BF16).astype(F32)
        pn = pnew_ref[0].astype(BF16).astype(F32)
        s1 = (jnp.sum(qlat.astype(F32) * cn, axis=1, keepdims=True)
              + jnp.sum(qpe.astype(F32) * pn, axis=1, keepdims=True))
        mml_ref[...] = s1
        lml_ref[...] = jnp.ones((_ROWS, 1), F32)
        aml_ref[...] = jnp.broadcast_to(cn, aml_ref.shape)

    cb = [cp[i][0].astype(BF16) for i in range(npg)]

    s_da = [jnp.where(own, _dot_nt(qrows, kp[i][0].astype(BF16)), NEG) for i in range(npg)]
    pos = (g * npg + (npg - 1)) * page + lax.shift_right_logical(col, 2)
    near = head_bias(_t5_bucket(past - pos))
    s_da[npg - 1] = s_da[npg - 1] + jnp.where(g == last, near, jnp.zeros_like(near))
    s_ml = [_dot_nt(qlat, cb[i]) + _dot(qpe, pp[i][0].astype(BF16)) for i in range(npg)]

    def update(s_list, m_ref, l_ref, a_ref, values):
        m_old = m_ref[...]
        m_new = m_old
        for s in s_list:
            m_new = jnp.maximum(m_new, jnp.max(s, axis=1, keepdims=True))
        alpha = jnp.exp2(m_old - m_new)
        l_new = alpha * l_ref[...]
        acc = alpha * a_ref[...]
        for s, v in zip(s_list, values):
            p = jnp.exp2(s - m_new)
            l_new = l_new + jnp.sum(p, axis=1, keepdims=True)
            acc = acc + _dot(p.astype(BF16), v)
        m_ref[...] = m_new
        l_ref[...] = l_new
        a_ref[...] = acc

    update(s_da, mda_ref, lda_ref, ada_ref, [vp[i][0].astype(BF16) for i in range(npg)])
    update(s_ml, mml_ref, lml_ref, aml_ref, cb)

    @pl.when(g == last)
    def _():
        lam = _lambda_full(lam_ref, lam_init)
        accn = ada_ref[...] * (1.0 / lda_ref[...])
        o = accn[:DA_HEADS] - lam * accn[SUBLANES:SUBLANES + DA_HEADS]
        oa_ref[0] = _rms(o, subg_ref[...]) * (1.0 - lam_init)
        olat_ref[0] = (aml_ref[...] * (1.0 / lml_ref[...]))[:MLA_HEADS]


def _sample_attention(lam_init, rel_bias, da_lambda, subg, page_table, q8, qlat, qpe, knew, vnew, cnew, pnew,
                      cache_k, cache_v, cache_c, cache_pt):
    n_seq, n_pages = page_table.shape
    page = cache_c.shape[1]
    kv_rank = cache_c.shape[2]
    npg = PAGES_PER_STEP
    assert n_pages % npg == 0 and page >= BIAS_CUTOFF
    past = n_pages * page

    def page_specs(shape):
        return [pl.BlockSpec((1,) + shape, functools.partial(lambda i, g, pt, p: (pt[i, g * npg + p], 0, 0), p=p))
                for p in range(npg)]

    seq_spec = lambda r, width: pl.BlockSpec((1, r, width), lambda i, g, pt: (i, 0, 0))
    smem = pl.BlockSpec(memory_space=pltpu.SMEM)
    const = lambda a: pl.BlockSpec(a.shape, lambda i, g, pt: (0,) * a.ndim)
    grid_spec = pltpu.PrefetchScalarGridSpec(
        num_scalar_prefetch=1,
        grid=(n_seq, n_pages // npg),
        in_specs=([smem, const(da_lambda), const(subg), seq_spec(SUBLANES, LANES), seq_spec(_ROWS, kv_rank),
                   seq_spec(_ROWS, MLA_ROPE), seq_spec(SUBLANES, LANES), seq_spec(SUBLANES, LANES),
                   seq_spec(1, kv_rank), seq_spec(1, MLA_ROPE)]
                  + page_specs((page * DA_HEADS, LANES)) + page_specs((page * DA_HEADS, LANES))
                  + page_specs((page, kv_rank)) + page_specs((MLA_ROPE, page))),
        out_specs=[seq_spec(DA_HEADS, LANES), seq_spec(MLA_HEADS, kv_rank)],
        scratch_shapes=[pltpu.VMEM((_ROWS, 1), F32), pltpu.VMEM((_ROWS, 1), F32), pltpu.VMEM((_ROWS, LANES), F32),
                        pltpu.VMEM((_ROWS, 1), F32), pltpu.VMEM((_ROWS, 1), F32), pltpu.VMEM((_ROWS, kv_rank), F32)],
    )
    return pl.pallas_call(
        functools.partial(_sattn_kernel, lam_init, past, page),
        grid_spec=grid_spec,
        out_shape=[jax.ShapeDtypeStruct((n_seq, DA_HEADS, LANES), F32),
                   jax.ShapeDtypeStruct((n_seq, MLA_HEADS, kv_rank), F32)],
        compiler_params=pltpu.CompilerParams(dimension_semantics=("arbitrary", "arbitrary"),
                                             vmem_limit_bytes=VMEM_LIMIT),
        name="sample_attention",
    )(page_table, rel_bias, da_lambda, subg, q8, qlat, qpe, knew, vnew, cnew, pnew,
      *([cache_k] * npg), *([cache_v] * npg), *([cache_c] * npg), *([cache_pt] * npg))


def _headproj_kernel(n_heads, in_w, out_w, x_ref, w_ref, o_ref):
    for hh in range(n_heads):
        x = x_ref[:, hh * in_w:(hh + 1) * in_w].astype(BF16)
        o_ref[:, hh * out_w:(hh + 1) * out_w] = _dot(x, w_ref[hh]).astype(o_ref.dtype)


def _headproj(x, w, out_dtype, name):
    n_heads, in_w, out_w = w.shape
    r = x.shape[0]
    return pl.pallas_call(
        functools.partial(_headproj_kernel, n_heads, in_w, out_w),
        out_shape=jax.ShapeDtypeStruct((r, n_heads * out_w), out_dtype),
        compiler_params=pltpu.CompilerParams(vmem_limit_bytes=VMEM_LIMIT),
        name=name,
    )(x, w)


def _post_kernel(d_ff, x_ref, oa_ref, ob_ref, gt_ref, gt1_ref, sh2_ref, sc2_ref, gt2_ref, ng_ref,
                 wa_ref, wb_ref, wo_ref, wgu_ref, wdn_ref, y_ref):
    d = x_ref.shape[1]
    x = x_ref[...]
    gates = gt_ref[...]
    merged = (gates[:, :d].astype(F32) * _dot(oa_ref[...], wa_ref[...])
              + gates[:, d:].astype(F32) * _dot(ob_ref[...], wb_ref[...]))
    x1 = x + gt1_ref[0] * _rms(_dot(merged.astype(BF16), wo_ref[...]), ng_ref[1:2])
    h2 = (_rms(x1, ng_ref[2:3]) * (1.0 + sc2_ref[0]) + sh2_ref[0]).astype(BF16)
    gg = _dot(h2, wgu_ref[:, :d_ff])
    uu = _dot(h2, wgu_ref[:, d_ff:])
    act = (gg * jax.nn.sigmoid(gg) * uu).astype(BF16)
    y_ref[...] = x1 + gt2_ref[0] * _rms(_dot(act, wdn_ref[...]), ng_ref[3:4])


def _post(x2d, oa, ob, gates, mods, w, tm, rows_per_mod_block):
    t, d = x2d.shape
    d_ff = w["w_dn"].shape[0]
    rmod = mods[0].shape[1]
    row = lambda width: pl.BlockSpec((tm, width), lambda i: (i, 0))
    mod_spec = pl.BlockSpec((1, rmod, d), lambda i: (i // rows_per_mod_block, 0, 0))
    weights = [w["ng"], w["w_a"], w["w_b"], w["w_o"], w["w_gu"], w["w_dn"]]
    return pl.pallas_call(
        functools.partial(_post_kernel, d_ff),
        grid=(t // tm,),
        in_specs=[row(d), row(oa.shape[1]), row(ob.shape[1]), row(2 * d)] + [mod_spec] * 4
                 + [_const_spec(a.shape) for a in weights],
        out_specs=row(d),
        out_shape=jax.ShapeDtypeStruct((t, d), F32),
        compiler_params=pltpu.CompilerParams(dimension_semantics=("arbitrary",), vmem_limit_bytes=VMEM_LIMIT),
        name="post",
    )(x2d, oa, ob, gates, *mods, *weights)


def _layer_weights(norm_g, w_in, mla_q_norm_g, mla_kv_norm_g, mla_w_uq, mla_w_uk, mla_w_uv, w_branch_a,
                   w_branch_b, w_o, ffn_w_gu, ffn_w_down):
    d = w_in.shape[0]
    q_rank = mla_q_norm_g.shape[0]
    kv_rank = mla_kv_norm_g.shape[0]
    half = MLA_ROPE // 2
    c0 = 3 * DA_W + q_rank + kv_rank
    w_kpe = w_in[:, c0:c0 + MLA_ROPE]
    partner = jnp.concatenate([w_kpe[:, half:], w_kpe[:, :half]], axis=1)
    w_main = jnp.concatenate([w_in[:, :c0], w_kpe, partner, jnp.zeros((d, MLA_ROPE), F32), partner], axis=1)
    w_g = w_in[:, c0 + MLA_ROPE:]

    pad = LANES - MLA_NOPE - MLA_ROPE
    zq = lambda n: jnp.zeros((q_rank, MLA_HEADS, n), F32)
    nope, r1, r2 = mla_w_uq[..., :MLA_NOPE], mla_w_uq[..., MLA_NOPE:MLA_NOPE + half], mla_w_uq[..., MLA_NOPE + half:]
    w_uq1 = jnp.concatenate([nope, r1, r2, zq(pad)], axis=-1).reshape(q_rank, MLA_W)
    w_uq2 = jnp.concatenate([zq(MLA_NOPE), r2, r1, zq(pad)], axis=-1).reshape(q_rank, MLA_W)
    zk = lambda n: jnp.zeros((kv_rank, MLA_HEADS, n), F32)
    w_ukp = jnp.concatenate([mla_w_uk, zk(LANES - MLA_NOPE)], axis=-1).reshape(kv_rank, MLA_W)
    e = np.zeros((LANES, MLA_HEADS, LANES), np.float32)
    for l in range(MLA_ROPE):
        e[l, :, MLA_NOPE + l] = 1.0
    e_kpe = jnp.asarray(e.reshape(LANES, MLA_W))
    w_uvt = mla_w_uv.reshape(kv_rank, MLA_VW).T
    w_lat = jnp.concatenate([jnp.transpose(mla_w_uk, (1, 2, 0)),
                             jnp.zeros((MLA_HEADS, LANES - MLA_NOPE, kv_rank), F32)], axis=1)
    w_val = jnp.transpose(mla_w_uv, (1, 0, 2))
    bf = lambda a: a.astype(BF16)
    return dict(
        g0=norm_g[0:1], ng=norm_g, gq=mla_q_norm_g[None], gkv=mla_kv_norm_g[None],
        w_main=bf(w_main), w_g=bf(w_g), w_uq1=bf(w_uq1), w_uq2=bf(w_uq2), w_ukp=bf(w_ukp), w_uvt=bf(w_uvt),
        e_kpe=bf(e_kpe), w_lat=bf(w_lat), w_val=bf(w_val),
        w_a=bf(w_branch_a), w_b=bf(w_branch_b), w_o=bf(w_o), w_gu=bf(ffn_w_gu), w_dn=bf(ffn_w_down))


def _rope_tables(pos):
    freqs = ROPE_THETA ** (-jnp.arange(0, MLA_ROPE, 2, dtype=F32) / MLA_ROPE)
    ang = pos.astype(F32)[:, None] * freqs[None, :]
    cos, sin = jnp.cos(ang), jnp.sin(ang)
    n = pos.shape[0]
    one = jnp.ones((n, MLA_NOPE), F32)
    zero = lambda w: jnp.zeros((n, w), F32)
    pad = LANES - MLA_NOPE - MLA_ROPE
    cosq = jnp.concatenate([one, cos, cos, zero(pad)], axis=1) * MLA_SCALE
    sinq = jnp.concatenate([zero(MLA_NOPE), -sin, sin, zero(pad)], axis=1) * MLA_SCALE
    part = jnp.concatenate([-sin, sin], axis=1)
    tabk = jnp.concatenate([cos, cos, part, zero(MLA_ROPE), part], axis=1)
    return cosq, sinq, tabk


def _layer(l, x_prompt, x_sample, cache_k, cache_v, cache_c, cache_p, page_table, c_prompt, c_sample, rel_bias,
           ada_w, ada_b, da_lambda, da_subln_g, w):
    batch, seq, d = x_prompt.shape
    n_seq, dec_seq, _ = x_sample.shape
    assert dec_seq == 1
    lam_init = 0.8 - 0.6 * math.exp(-0.3 * l)
    n_pool, page = cache_k.shape[1:3]
    past = page_table.shape[1] * page
    subg = da_subln_g[None]

    mod = _adaln(jnp.concatenate([c_prompt, c_sample], axis=0), ada_w, ada_b)
    mods_p = [mod[:batch, i * d:(i + 1) * d].reshape(batch, 1, d) for i in range(6)]
    mods_s = [mod[batch:, i * d:(i + 1) * d].reshape(1, n_seq, d) for i in range(6)]

    xp = x_prompt.reshape(batch * seq, d)
    tm = TILE
    (qd, kd, vd, kdb, vdt, ckv, kpe, qm, km, vmt, gates) = _inproj(
        xp, mods_p[1], mods_p[0], w, _rope_tables(jnp.arange(seq, dtype=jnp.int32)), tm, seq // tm)
    oa, ob = _prompt_attention(lam_init, rel_bias, da_lambda, subg, qd, qm, kdb, vdt, km, vmt, batch, seq)
    y_p = _post(xp, oa, ob, gates, [mods_p[2], mods_p[3], mods_p[4], mods_p[5]], w, tm, seq // tm)
    state_p = (kd.reshape(batch, seq, DA_HEADS, 2 * DA_DH), vd.reshape(batch, seq, DA_HEADS, DA_VD),
               ckv.reshape(batch, seq, -1), kpe.reshape(batch, seq, MLA_ROPE))

    xs = x_sample.reshape(n_seq, d)
    pos_s = jnp.full((n_seq,), past, jnp.int32)
    (qd_s, kd_s, vd_s, _, _, ckv_s, kpe_s, qm_s, _, _, gates_s) = _inproj(
        xs, mods_s[1], mods_s[0], w, _rope_tables(pos_s), n_seq, 1)
    kv_rank = ckv_s.shape[1]
    qlat = _headproj(qm_s, w["w_lat"], BF16, "latent_query").reshape(n_seq, MLA_HEADS, kv_rank)
    qpe = qm_s.reshape(n_seq, MLA_HEADS, LANES)[:, :, MLA_NOPE:MLA_NOPE + MLA_ROPE]
    pad_rows = lambda a, r: jnp.pad(a, ((0, 0), (0, r - a.shape[1]), (0, 0)))
    oa_s, olat = _sample_attention(
        lam_init, rel_bias, da_lambda, subg, page_table,
        pad_rows(qd_s.reshape(n_seq, DA_HEADS, LANES), SUBLANES), pad_rows(qlat, _ROWS), pad_rows(qpe, _ROWS),
        pad_rows(kd_s, SUBLANES), pad_rows(vd_s, SUBLANES), ckv_s.reshape(n_seq, 1, kv_rank),
        kpe_s.reshape(n_seq, 1, MLA_ROPE),
        cache_k.reshape(n_pool, page * DA_HEADS, LANES), cache_v.reshape(n_pool, page * DA_HEADS, LANES),
        cache_c.reshape(n_pool, page, kv_rank), jnp.swapaxes(cache_p.reshape(n_pool, page, MLA_ROPE), 1, 2))
    ob_s = _headproj(olat.reshape(n_seq, MLA_HEADS * kv_rank), w["w_val"], BF16, "value_up")
    y_s = _post(xs, oa_s.reshape(n_seq, DA_W).astype(BF16), ob_s, gates_s,
                [mods_s[2], mods_s[3], mods_s[4], mods_s[5]], w, n_seq, 1)
    state_s = (kd_s.reshape(n_seq, 1, DA_HEADS, 2 * DA_DH), vd_s.reshape(n_seq, 1, DA_HEADS, DA_VD),
               ckv_s.reshape(n_seq, 1, -1), kpe_s.reshape(n_seq, 1, MLA_ROPE))
    return y_p.reshape(batch, seq, d), y_s.reshape(n_seq, 1, d), state_p, state_s


def kernel(x_prompt, x_sample, cache_da_k, cache_da_v, cache_mla_ckv, cache_mla_kpe, page_table, c_prompt, c_sample, rel_bias, ada_w, ada_b, norm_g, w_in, da_lambda, da_subln_g, mla_q_norm_g, mla_kv_norm_g, mla_w_uq, mla_w_uk, mla_w_uv, w_branch_a, w_branch_b, w_o, ffn_w_gu, ffn_w_down):
    depth = ada_w.shape[0]
    y_p, y_s = x_prompt, x_sample
    st_p, st_s = [], []
    for l in range(depth):
        w = _layer_weights(norm_g[l], w_in[l], mla_q_norm_g[l], mla_kv_norm_g[l], mla_w_uq[l], mla_w_uk[l],
                           mla_w_uv[l], w_branch_a[l], w_branch_b[l], w_o[l], ffn_w_gu[l], ffn_w_down[l])
        y_p, y_s, sp, ss = _layer(l, y_p, y_s, cache_da_k[l:l + 1], cache_da_v[l:l + 1], cache_mla_ckv[l:l + 1],
                                  cache_mla_kpe[l:l + 1], page_table, c_prompt, c_sample, rel_bias, ada_w[l],
                                  ada_b[l], da_lambda[l], da_subln_g[l], w)
        st_p.append(sp)
        st_s.append(ss)
    stack = lambda sts, i: jnp.stack([s[i] for s in sts])
    return (y_p, y_s, stack(st_p, 0), stack(st_p, 1), stack(st_p, 2), stack(st_p, 3),
            stack(st_s, 0), stack(st_s, 1), stack(st_s, 2), stack(st_s, 3))
```

```python
import functools
import math

import numpy as np
import jax
import jax.numpy as jnp
from jax import lax
from jax.experimental import pallas as pl
from jax.experimental.pallas import tpu as pltpu

F32 = jnp.float32
BF16 = jnp.bfloat16

DA_HEADS = 4
DA_DH = 64
DA_VD = 2 * DA_DH
MLA_HEADS = 8
MLA_NOPE = 64
MLA_ROPE = 32
MLA_VD = 64
ROPE_THETA = 10000.0
REL_BUCKETS = 32
REL_MAX_DIST = 128
EPS = 1e-6
LOG2E = math.log2(math.e)
DA_SCALE = DA_DH ** -0.5 * LOG2E
MLA_SCALE = (MLA_NOPE + MLA_ROPE) ** -0.5 * LOG2E

LANES = 128
SUBLANES = 8
DA_W = DA_HEADS * LANES
MLA_W = MLA_HEADS * LANES
MLA_VW = MLA_HEADS * MLA_VD
BIAS_CUTOFF = 113
NEG = -1e30
VMEM_LIMIT = 56 * 1024 * 1024

TILE = 256
N_MAPS = 2 * DA_HEADS + MLA_HEADS
CHUNK_PAGES = 16
SUB_PAGES = 4
CHUNK_SLOTS = 2
ROWS = 16

NT_DIMS = (((1,), (1,)), ((), ()))


def _rms(x, g):
    return x * lax.rsqrt(jnp.mean(x * x, axis=-1, keepdims=True) + EPS) * g


def _dot(a, b):
    return jnp.dot(a, b, preferred_element_type=F32)


def _dot_nt(a, b):
    return lax.dot_general(a, b, NT_DIMS, preferred_element_type=F32)


def _const_spec(shape):
    nd = len(shape)
    return pl.BlockSpec(shape, lambda *_: (0,) * nd)


def _adaln_kernel(c_ref, w_ref, b_ref, o_ref):
    c = c_ref[...]
    a = (c * jax.nn.sigmoid(c)).astype(BF16)
    o_ref[...] = _dot(a, w_ref[...].astype(BF16)) + b_ref[...]


def _adaln(c_all, ada_w, ada_b):
    r, d = c_all.shape
    n = ada_w.shape[1]
    tn = 768
    return pl.pallas_call(
        _adaln_kernel,
        grid=(n // tn,),
        in_specs=[pl.BlockSpec((r, d), lambda j: (0, 0)),
                  pl.BlockSpec((d, tn), lambda j: (0, j)),
                  pl.BlockSpec((1, tn), lambda j: (0, j))],
        out_specs=pl.BlockSpec((r, tn), lambda j: (0, j)),
        out_shape=jax.ShapeDtypeStruct((r, n), F32),
        compiler_params=pltpu.CompilerParams(dimension_semantics=("arbitrary",), vmem_limit_bytes=VMEM_LIMIT),
        name="adaln",
    )(c_all, ada_w, ada_b.reshape(1, n))


_C_QD, _C_KD, _C_VD = 0, DA_W, 2 * DA_W
_C_CQ = 3 * DA_W


def _inproj_kernel(q_rank, kv_rank,
                   x_ref, sc_ref, sh_ref, g0_ref, wmain_ref, wg_ref, gq_ref, gkv_ref,
                   wuq1_ref, wuq2_ref, wukp_ref, wuvt_ref, ekpe_ref, cosq_ref, sinq_ref, tabk_ref,
                   qd_ref, kd_ref, vd_ref, kdb_ref, vdt_ref, ckv_ref, kpe_ref, qm_ref, km_ref, vmt_ref, gt_ref):
    c_ckv = _C_CQ + q_rank
    c_kpe = c_ckv + kv_rank
    h = _rms(x_ref[...], g0_ref[...]) * (1.0 + sc_ref[0]) + sh_ref[0]
    hb = h.astype(BF16)

    def mm(lo, hi):
        return _dot(hb, wmain_ref[:, lo:hi])

    qd_ref[...] = (mm(_C_QD, _C_KD) * DA_SCALE).astype(BF16)
    kd = mm(_C_KD, _C_VD)
    vd = mm(_C_VD, _C_CQ)
    for hh in range(DA_HEADS):
        cols = slice(hh * LANES, (hh + 1) * LANES)
        kd_ref[:, hh, :] = kd[:, cols]
        vd_ref[:, hh, :] = vd[:, cols]
    kdb_ref[...] = kd.astype(BF16)
    vdt_ref[0] = vd.T.astype(BF16)

    cq = _rms(mm(_C_CQ, c_ckv), gq_ref[...]).astype(BF16)
    q1 = _dot(cq, wuq1_ref[...])
    q2 = _dot(cq, wuq2_ref[...])
    cosq = cosq_ref[...]
    sinq = sinq_ref[...]
    for hh in range(MLA_HEADS):
        sl = slice(hh * LANES, (hh + 1) * LANES)
        qm_ref[:, sl] = (q1[:, sl] * cosq + q2[:, sl] * sinq).astype(BF16)

    ckv = _rms(mm(c_ckv, c_kpe), gkv_ref[...])
    ckv_ref[...] = ckv
    cb = ckv.astype(BF16)
    r = mm(c_kpe, c_kpe + LANES) * tabk_ref[...]
    kr = r + pltpu.roll(r, 3 * LANES // 4, 1)
    kpe_ref[...] = kr[:, :MLA_ROPE]
    km_ref[...] = (_dot(cb, wukp_ref[...]) + _dot(kr.astype(BF16), ekpe_ref[...])).astype(BF16)
    vmt_ref[0] = _dot_nt(wuvt_ref[...], cb).astype(BF16)

    gt_ref[...] = jax.nn.sigmoid(_dot(hb, wg_ref[...])).astype(BF16)


def _inproj(x2d, sc, sh, w, tabs, tm, rows_per_mod_block):
    t, d = x2d.shape
    q_rank = w["gq"].shape[1]
    kv_rank = w["gkv"].shape[1]
    rmod = sc.shape[1]
    n_tab_blocks = tabs[0].shape[0] // tm
    n_tiles = t // tm
    row = lambda width: pl.BlockSpec((tm, width), lambda i: (i, 0))
    heads = pl.BlockSpec((tm, DA_HEADS, LANES), lambda i: (i, 0, 0))
    transposed = lambda width: pl.BlockSpec((1, width, tm), lambda i: (i, 0, 0))
    mod_spec = pl.BlockSpec((1, rmod, d), lambda i: (i // rows_per_mod_block, 0, 0))
    tab_spec = pl.BlockSpec((tm, LANES), lambda i: (i % n_tab_blocks, 0))
    weights = [w["g0"], w["w_main"], w["w_g"], w["gq"], w["gkv"], w["w_uq1"], w["w_uq2"], w["w_ukp"], w["w_uvt"],
               w["e_kpe"]]
    in_specs = ([row(d), mod_spec, mod_spec] + [_const_spec(a.shape) for a in weights] + [tab_spec] * 3)
    sds = jax.ShapeDtypeStruct
    outs = [
        (row(DA_W), sds((t, DA_W), BF16)),
        (heads, sds((t, DA_HEADS, LANES), F32)),
        (heads, sds((t, DA_HEADS, LANES), F32)),
        (row(DA_W), sds((t, DA_W), BF16)),
        (transposed(DA_W), sds((n_tiles, DA_W, tm), BF16)),
        (row(kv_rank), sds((t, kv_rank), F32)),
        (row(MLA_ROPE), sds((t, MLA_ROPE), F32)),
        (row(MLA_W), sds((t, MLA_W), BF16)),
        (row(MLA_W), sds((t, MLA_W), BF16)),
        (transposed(MLA_VW), sds((n_tiles, MLA_VW, tm), BF16)),
        (row(2 * d), sds((t, 2 * d), BF16)),
    ]
    return pl.pallas_call(
        functools.partial(_inproj_kernel, q_rank, kv_rank),
        grid=(n_tiles,),
        in_specs=in_specs,
        out_specs=[o[0] for o in outs],
        out_shape=[o[1] for o in outs],
        compiler_params=pltpu.CompilerParams(dimension_semantics=("arbitrary",), vmem_limit_bytes=VMEM_LIMIT),
        name="inproj",
    )(x2d, sc, sh, *weights, *tabs)


def _t5_bucket(dist):
    n = jnp.maximum(dist, 0)
    max_exact = REL_BUCKETS // 2
    nf = jnp.maximum(n, max_exact).astype(F32)
    large = max_exact + (jnp.log(nf / max_exact) / math.log(REL_MAX_DIST / max_exact)
                         * (REL_BUCKETS - max_exact)).astype(jnp.int32)
    large = jnp.minimum(large, REL_BUCKETS - 1)
    return jnp.where(n < max_exact, n, large)


def _shifted_bias(relb_ref, bucket, head):
    last = relb_ref[REL_BUCKETS - 1, head]
    out = jnp.zeros(bucket.shape, F32)
    for b in range(REL_BUCKETS - 1):
        out = jnp.where(bucket == b, (relb_ref[b, head] - last) * LOG2E, out)
    return out


def _lambda_full(lam_ref, lam_init):
    lf = lam_ref[...]
    a = jnp.sum(lf[0:1] * lf[1:2], axis=1, keepdims=True)
    b = jnp.sum(lf[2:3] * lf[3:4], axis=1, keepdims=True)
    return jnp.exp(a) - jnp.exp(b) + lam_init


def _attn_kernel(lam_init, past, page, n_groups, n_chunks, chunks_per_step,
                 pt_ref,
                 relb_ref, lam_ref, subg_ref, qd_ref, qm_ref, kd_ref, vdt_ref, km_ref, vmt_ref,
                 q8_ref, qlat_ref, qpe_ref, knew_ref, vnew_ref, cnew_ref, pnew_ref,
                 ck_hbm, cv_hbm, cc_hbm, cp_hbm,
                 oa_ref, ob_ref, oas_ref, olat_ref,
                 bias_ref, qh_ref, m_ref, l_ref, acc_da_ref, acc_ml_ref,
                 kbuf, vbuf, cbuf, pbuf, sem, mda_ref, lda_ref, ada_ref, mml_ref, lml_ref, aml_ref, near_ref):
    t = TILE
    bi = pl.program_id(0)
    qi = pl.program_id(1)
    nq = pl.num_programs(1)
    n_da = 2 * DA_HEADS
    npg = CHUNK_PAGES
    lam = _lambda_full(lam_ref, lam_init)
    rows_pair = page * DA_HEADS // 2
    row1 = lax.broadcasted_iota(jnp.int32, (ROWS, LANES), 0)
    lane1 = lax.broadcasted_iota(jnp.int32, (ROWS, LANES), 1)
    rowc = lax.broadcasted_iota(jnp.int32, (ROWS, rows_pair), 0)
    colc = lax.broadcasted_iota(jnp.int32, (ROWS, rows_pair), 1)
    own = (colc & 1) == lax.shift_right_logical(rowc & (SUBLANES - 1), 1)

    def chunk_copies(c, slot):
        seq = lax.div(c, n_groups)
        first = lax.rem(c, n_groups) * npg
        copies = []
        for p in range(npg):
            pg = pt_ref[seq, first + p]
            copies += [pltpu.make_async_copy(ck_hbm.at[pg], kbuf.at[slot, p], sem.at[slot]),
                       pltpu.make_async_copy(cv_hbm.at[pg], vbuf.at[slot, p], sem.at[slot]),
                       pltpu.make_async_copy(cc_hbm.at[pg], cbuf.at[slot, p], sem.at[slot]),
                       pltpu.make_async_copy(cp_hbm.at[pg], pbuf.at[slot, p], sem.at[slot])]
        return copies

    def head_bias(bucket):
        out = jnp.zeros(bucket.shape, F32)
        r = lax.broadcasted_iota(jnp.int32, bucket.shape, 0) & (SUBLANES - 1)
        for hh in range(DA_HEADS):
            out = jnp.where(r == hh, _shifted_bias(relb_ref, bucket, hh), out)
        return out

    def softmax_update(s_list, mx_ref, sum_ref, a_ref, values):
        m_old = mx_ref[...]
        m_new = m_old
        for s in s_list:
            m_new = jnp.maximum(m_new, jnp.max(s, axis=1, keepdims=True))
        alpha = jnp.exp2(m_old - m_new)
        l_new = alpha * sum_ref[...]
        acc = alpha * a_ref[...]
        for s, v in zip(s_list, values):
            p = jnp.exp2(s - m_new)
            l_new = l_new + jnp.sum(p, axis=1, keepdims=True)
            acc = acc + _dot(p.astype(BF16), v)
        mx_ref[...] = m_new
        sum_ref[...] = l_new
        a_ref[...] = acc

    def consume_chunk(c):
        slot = lax.rem(c, CHUNK_SLOTS)
        for cp in chunk_copies(c, slot):
            cp.wait()
        seq = lax.div(c, n_groups)
        grp = lax.rem(c, n_groups)
        q8 = q8_ref[seq].astype(F32)
        q16 = jnp.concatenate([q8, q8], axis=0)
        qhalf = jnp.where(lax.shift_right_logical(lane1, 6) == lax.shift_right_logical(row1, 3), q16, 0.0)
        qrows = jnp.concatenate([jnp.where((row1 & 1) == 0, qhalf, 0.0), jnp.where((row1 & 1) == 1, qhalf, 0.0)],
                                axis=1).astype(BF16)
        qlat = qlat_ref[seq]
        qpe = qpe_ref[seq]

        @pl.when(grp == 0)
        def _():
            kn = knew_ref[seq].astype(BF16).astype(F32)
            kn = jnp.concatenate([kn, kn], axis=0)
            mda_ref[...] = (jnp.sum(qhalf.astype(BF16).astype(F32) * kn, axis=1, keepdims=True)
                            + head_bias(jnp.zeros((ROWS, 1), jnp.int32)))
            lda_ref[...] = jnp.ones((ROWS, 1), F32)
            vn = vnew_ref[seq].astype(BF16).astype(F32)
            vn = jnp.concatenate([vn, vn], axis=0)
            ada_ref[...] = jnp.concatenate([vn, vn], axis=1)
            cn = cnew_ref[pl.ds(seq, 1), :].astype(BF16).astype(F32)
            pn = pnew_ref[pl.ds(seq, 1), :].astype(BF16).astype(F32)
            mml_ref[...] = (jnp.sum(qlat.astype(F32) * cn, axis=1, keepdims=True)
                            + jnp.sum(qpe.astype(F32) * pn, axis=1, keepdims=True))
            lml_ref[...] = jnp.ones((ROWS, 1), F32)
            aml_ref[...] = jnp.broadcast_to(cn, aml_ref.shape)

        def paired(buf, i):
            even = buf[slot, i, pl.ds(0, rows_pair, stride=2), :]
            odd = buf[slot, i, pl.ds(1, rows_pair, stride=2), :]
            return jnp.concatenate([even, odd], axis=1).astype(BF16)

        halves = [range(k, k + SUB_PAGES) for k in range(0, npg, SUB_PAGES)]
        cb = [cbuf[slot, i].astype(BF16) for i in range(npg)]
        s_da, s_ml = {}, {}
        for pages in halves:
            for i in pages:
                s_da[i] = jnp.where(own, _dot_nt(qrows, paired(kbuf, i)), NEG)
                s_ml[i] = _dot_nt(qlat, cb[i]) + _dot(qpe, pbuf[slot, i].astype(BF16))
        near = near_ref[...]
        s_da[npg - 1] = s_da[npg - 1] + jnp.where(grp == n_groups - 1, near, jnp.zeros_like(near))
        for pages in halves:
            softmax_update([s_da[i] for i in pages], mda_ref, lda_ref, ada_ref, [paired(vbuf, i) for i in pages])
            softmax_update([s_ml[i] for i in pages], mml_ref, lml_ref, aml_ref, [cb[i] for i in pages])

        @pl.when(grp == n_groups - 1)
        def _():
            accn = ada_ref[...] * (1.0 / lda_ref[...])
            accn = jnp.where((row1 & 1) == 0, accn[:, :LANES], accn[:, LANES:])
            o = accn[:DA_HEADS] - lam * accn[SUBLANES:SUBLANES + DA_HEADS]
            oas_ref[seq] = _rms(o, subg_ref[...]) * (1.0 - lam_init)
            olat_ref[seq] = (aml_ref[...] * (1.0 / lml_ref[...]))[:MLA_HEADS]

        nxt = c + CHUNK_SLOTS

        @pl.when(nxt < n_chunks)
        def _():
            for cp in chunk_copies(nxt, slot):
                cp.start()

    def consume_chunks(step_ordinal):
        for u in range(chunks_per_step):
            c = step_ordinal * chunks_per_step + u

            @pl.when(c < n_chunks)
            def _():
                consume_chunk(c)

    @pl.when((bi == 0) & (qi == 0))
    def _():
        for c in range(min(CHUNK_SLOTS, n_chunks)):
            for cp in chunk_copies(c, c):
                cp.start()
        last_page_pos = (n_groups * npg - 1) * page + lax.shift_right_logical(colc, 1)
        near_ref[...] = head_bias(_t5_bucket(past - last_page_pos))

    @pl.when((bi == 0) & (qi == 0))
    def _():
        key = lax.broadcasted_iota(jnp.int32, (t, t), 0)
        qry = lax.broadcasted_iota(jnp.int32, (t, t), 1)
        d0 = qry - key
        b0 = _t5_bucket(d0)
        b1 = _t5_bucket(d0 + t)
        for hh in range(DA_HEADS):
            bias_ref[hh] = jnp.where(d0 >= 0, _shifted_bias(relb_ref, b0, hh), NEG)
            bias_ref[DA_HEADS + hh] = _shifted_bias(relb_ref, b1, hh)
        bias_ref[2 * DA_HEADS] = jnp.where(d0 >= 0, 0.0, NEG)

    lane = lax.broadcasted_iota(jnp.int32, (t, LANES), 1)
    for hh in range(DA_HEADS):
        q = qd_ref[:, hh * LANES:(hh + 1) * LANES].astype(F32)
        qh_ref[2 * hh] = jnp.where(lane < DA_DH, q, 0.0).astype(BF16)
        qh_ref[2 * hh + 1] = jnp.where(lane >= DA_DH, q, 0.0).astype(BF16)

    m_ref[...] = jnp.full(m_ref.shape, NEG, F32)
    l_ref[...] = jnp.zeros(l_ref.shape, F32)
    acc_da_ref[...] = jnp.zeros(acc_da_ref.shape, F32)
    acc_ml_ref[...] = jnp.zeros(acc_ml_ref.shape, F32)

    first_step = bi * ((nq * (nq + 1)) // 2) + (qi * (qi + 1)) // 2

    def step(j, kind):
        consume_chunks(first_step + j)
        scores = []
        for i in range(N_MAPS):
            if i < n_da:
                hh = i // 2
                k = kd_ref[0, j, :, hh * LANES:(hh + 1) * LANES]
                q = qh_ref[i]
                bias = {"far": None, "near": DA_HEADS + hh, "diag": hh}[kind]
            else:
                hh = i - n_da
                k = km_ref[0, j, :, hh * LANES:(hh + 1) * LANES]
                q = qm_ref[:, hh * LANES:(hh + 1) * LANES]
                bias = {"far": None, "near": None, "diag": 2 * DA_HEADS}[kind]
            s = _dot_nt(k, q)
            if bias is not None:
                s = s + bias_ref[bias]
            scores.append(s)
        probs = []
        for i in range(N_MAPS):
            s = scores[i]
            m_old = m_ref[i]
            m_new = jnp.maximum(m_old, jnp.max(s, axis=0, keepdims=True))
            alpha = jnp.exp2(m_old - m_new)
            p = jnp.exp2(s - m_new)
            l_ref[i] = alpha * l_ref[i] + jnp.sum(p, axis=0, keepdims=True)
            m_ref[i] = m_new
            probs.append((alpha, p.astype(BF16)))
        for i in range(N_MAPS):
            alpha, p = probs[i]
            if i < n_da:
                hh = i // 2
                vt = vdt_ref[j, hh * LANES:(hh + 1) * LANES, :]
                acc_ref, ai = acc_da_ref, i
            else:
                hh = i - n_da
                vt = vmt_ref[j, hh * MLA_VD:(hh + 1) * MLA_VD, :]
                acc_ref, ai = acc_ml_ref, hh
            acc_ref[ai] = alpha * acc_ref[ai] + _dot(vt, p)

    def far(j, carry):
        step(j, "far")
        return carry

    lax.fori_loop(0, qi - 1, far, 0)

    @pl.when(qi >= 1)
    def _():
        step(qi - 1, "near")

    step(qi, "diag")

    for hh in range(DA_HEADS):
        ot = (acc_da_ref[2 * hh] * (1.0 / l_ref[2 * hh])
              - lam * (acc_da_ref[2 * hh + 1] * (1.0 / l_ref[2 * hh + 1])))
        oa_ref[:, hh * LANES:(hh + 1) * LANES] = (_rms(ot.T, subg_ref[...]) * (1.0 - lam_init)).astype(BF16)
    for pair in range(MLA_HEADS // 2):
        ha, hb = 2 * pair, 2 * pair + 1
        ot = jnp.concatenate([acc_ml_ref[ha] * (1.0 / l_ref[n_da + ha]),
                              acc_ml_ref[hb] * (1.0 / l_ref[n_da + hb])], axis=0)
        ob_ref[:, pair * LANES:(pair + 1) * LANES] = ot.T.astype(BF16)


def _attention(lam_init, rel_bias, da_lambda, subg, qd, qm, kdb, vdt, km, vmt, batch, seq,
               page_table, q8, qlat, qpe, knew, vnew, cnew, pnew, cache_k, cache_v, cache_c, cache_pt):
    t = TILE
    nq = seq // t
    n_seq, n_pages = page_table.shape
    page = cache_c.shape[1]
    kv_rank = cache_c.shape[2]
    npg = CHUNK_PAGES
    assert n_pages % npg == 0 and page >= BIAS_CUTOFF
    n_groups = n_pages // npg
    n_chunks = n_seq * n_groups
    n_steps = batch * (nq * (nq + 1)) // 2
    chunks_per_step = -(-n_chunks // n_steps)
    past = n_pages * page

    tiles = lambda a: a.reshape(batch, nq, t, a.shape[-1])
    once = pl.Buffered(1)
    kv_spec = lambda width: pl.BlockSpec((1, nq, t, width), lambda b, i, pt: (b, 0, 0, 0), pipeline_mode=once)
    vt_spec = lambda width: pl.BlockSpec((nq, width, t), lambda b, i, pt: (b, 0, 0), pipeline_mode=once)
    q_spec = lambda width: pl.BlockSpec((t, width), lambda b, i, pt: (b * nq + i, 0))
    whole = lambda a: pl.BlockSpec(a.shape, lambda b, i, pt: (0,) * a.ndim, pipeline_mode=once)
    hbm = pl.BlockSpec(memory_space=pl.ANY)
    sds = jax.ShapeDtypeStruct
    out_shape = [sds((batch * seq, DA_W), BF16), sds((batch * seq, MLA_VW), BF16),
                 sds((n_seq, DA_HEADS, LANES), F32), sds((n_seq, MLA_HEADS, kv_rank), F32)]
    grid_spec = pltpu.PrefetchScalarGridSpec(
        num_scalar_prefetch=1,
        grid=(batch, nq),
        in_specs=[pl.BlockSpec(memory_space=pltpu.SMEM), whole(da_lambda), whole(subg),
                  q_spec(DA_W), q_spec(MLA_W), kv_spec(DA_W), vt_spec(DA_W), kv_spec(MLA_W), vt_spec(MLA_VW),
                  whole(q8), whole(qlat), whole(qpe), whole(knew), whole(vnew), whole(cnew), whole(pnew),
                  hbm, hbm, hbm, hbm],
        out_specs=[q_spec(DA_W), q_spec(MLA_VW),
                   pl.BlockSpec(out_shape[2].shape, lambda b, i, pt: (0, 0, 0)),
                   pl.BlockSpec(out_shape[3].shape, lambda b, i, pt: (0, 0, 0))],
        scratch_shapes=[pltpu.VMEM((2 * DA_HEADS + 1, t, t), F32), pltpu.VMEM((2 * DA_HEADS, t, LANES), BF16),
                        pltpu.VMEM((N_MAPS, 1, t), F32), pltpu.VMEM((N_MAPS, 1, t), F32),
                        pltpu.VMEM((2 * DA_HEADS, LANES, t), F32), pltpu.VMEM((MLA_HEADS, MLA_VD, t), F32),
                        pltpu.VMEM((CHUNK_SLOTS, npg, page * DA_HEADS, LANES), F32),
                        pltpu.VMEM((CHUNK_SLOTS, npg, page * DA_HEADS, LANES), F32),
                        pltpu.VMEM((CHUNK_SLOTS, npg, page, kv_rank), F32),
                        pltpu.VMEM((CHUNK_SLOTS, npg, MLA_ROPE, page), F32),
                        pltpu.SemaphoreType.DMA((CHUNK_SLOTS,)),
                        pltpu.VMEM((ROWS, 1), F32), pltpu.VMEM((ROWS, 1), F32), pltpu.VMEM((ROWS, 2 * LANES), F32),
                        pltpu.VMEM((ROWS, 1), F32), pltpu.VMEM((ROWS, 1), F32), pltpu.VMEM((ROWS, kv_rank), F32),
                        pltpu.VMEM((ROWS, page * DA_HEADS // 2), F32)],
    )
    return pl.pallas_call(
        functools.partial(_attn_kernel, lam_init, past, page, n_groups, n_chunks, chunks_per_step),
        grid_spec=grid_spec,
        out_shape=out_shape,
        compiler_params=pltpu.CompilerParams(dimension_semantics=("arbitrary", "arbitrary"),
                                             vmem_limit_bytes=VMEM_LIMIT),
        name="attention",
    )(page_table, rel_bias, da_lambda, subg, qd, qm, tiles(kdb), vdt, tiles(km), vmt,
      q8, qlat, qpe, knew, vnew, cnew, pnew, cache_k, cache_v, cache_c, cache_pt)


def _headproj_kernel(n_heads, in_w, out_w, x_ref, w_ref, o_ref):
    for hh in range(n_heads):
        x = x_ref[:, hh * in_w:(hh + 1) * in_w].astype(BF16)
        o_ref[:, hh * out_w:(hh + 1) * out_w] = _dot(x, w_ref[hh]).astype(o_ref.dtype)


def _headproj(x, w, out_dtype, name):
    n_heads, in_w, out_w = w.shape
    r = x.shape[0]
    return pl.pallas_call(
        functools.partial(_headproj_kernel, n_heads, in_w, out_w),
        out_shape=jax.ShapeDtypeStruct((r, n_heads * out_w), out_dtype),
        compiler_params=pltpu.CompilerParams(vmem_limit_bytes=VMEM_LIMIT),
        name=name,
    )(x, w)


def _post_kernel(d_ff, x_ref, oa_ref, ob_ref, gt_ref, gt1_ref, sh2_ref, sc2_ref, gt2_ref, ng_ref,
                 wa_ref, wb_ref, wo_ref, wgu_ref, wdn_ref, y_ref):
    d = x_ref.shape[1]
    x = x_ref[...]
    gates = gt_ref[...]
    merged = (gates[:, :d].astype(F32) * _dot(oa_ref[...], wa_ref[...])
              + gates[:, d:].astype(F32) * _dot(ob_ref[...], wb_ref[...]))
    x1 = x + gt1_ref[0] * _rms(_dot(merged.astype(BF16), wo_ref[...]), ng_ref[1:2])
    h2 = (_rms(x1, ng_ref[2:3]) * (1.0 + sc2_ref[0]) + sh2_ref[0]).astype(BF16)
    gg = _dot(h2, wgu_ref[:, :d_ff])
    uu = _dot(h2, wgu_ref[:, d_ff:])
    act = (gg * jax.nn.sigmoid(gg) * uu).astype(BF16)
    y_ref[...] = x1 + gt2_ref[0] * _rms(_dot(act, wdn_ref[...]), ng_ref[3:4])


def _post(x2d, oa, ob, gates, mods, w, tm, rows_per_mod_block):
    t, d = x2d.shape
    d_ff = w["w_dn"].shape[0]
    rmod = mods[0].shape[1]
    row = lambda width: pl.BlockSpec((tm, width), lambda i: (i, 0))
    mod_spec = pl.BlockSpec((1, rmod, d), lambda i: (i // rows_per_mod_block, 0, 0))
    weights = [w["ng"], w["w_a"], w["w_b"], w["w_o"], w["w_gu"], w["w_dn"]]
    return pl.pallas_call(
        functools.partial(_post_kernel, d_ff),
        grid=(t // tm,),
        in_specs=[row(d), row(oa.shape[1]), row(ob.shape[1]), row(2 * d)] + [mod_spec] * 4
                 + [_const_spec(a.shape) for a in weights],
        out_specs=row(d),
        out_shape=jax.ShapeDtypeStruct((t, d), F32),
        compiler_params=pltpu.CompilerParams(dimension_semantics=("arbitrary",), vmem_limit_bytes=VMEM_LIMIT),
        name="post",
    )(x2d, oa, ob, gates, *mods, *weights)


def _layer_weights(norm_g, w_in, mla_q_norm_g, mla_kv_norm_g, mla_w_uq, mla_w_uk, mla_w_uv, w_branch_a,
                   w_branch_b, w_o, ffn_w_gu, ffn_w_down):
    d = w_in.shape[0]
    q_rank = mla_q_norm_g.shape[0]
    kv_rank = mla_kv_norm_g.shape[0]
    half = MLA_ROPE // 2
    c0 = 3 * DA_W + q_rank + kv_rank
    w_kpe = w_in[:, c0:c0 + MLA_ROPE]
    partner = jnp.concatenate([w_kpe[:, half:], w_kpe[:, :half]], axis=1)
    w_main = jnp.concatenate([w_in[:, :c0], w_kpe, partner, jnp.zeros((d, MLA_ROPE), F32), partner], axis=1)
    w_g = w_in[:, c0 + MLA_ROPE:]

    pad = LANES - MLA_NOPE - MLA_ROPE
    zq = lambda n: jnp.zeros((q_rank, MLA_HEADS, n), F32)
    nope, r1, r2 = mla_w_uq[..., :MLA_NOPE], mla_w_uq[..., MLA_NOPE:MLA_NOPE + half], mla_w_uq[..., MLA_NOPE + half:]
    w_uq1 = jnp.concatenate([nope, r1, r2, zq(pad)], axis=-1).reshape(q_rank, MLA_W)
    w_uq2 = jnp.concatenate([zq(MLA_NOPE), r2, r1, zq(pad)], axis=-1).reshape(q_rank, MLA_W)
    zk = lambda n: jnp.zeros((kv_rank, MLA_HEADS, n), F32)
    w_ukp = jnp.concatenate([mla_w_uk, zk(LANES - MLA_NOPE)], axis=-1).reshape(kv_rank, MLA_W)
    e = np.zeros((LANES, MLA_HEADS, LANES), np.float32)
    for l in range(MLA_ROPE):
        e[l, :, MLA_NOPE + l] = 1.0
    e_kpe = jnp.asarray(e.reshape(LANES, MLA_W))
    w_uvt = mla_w_uv.reshape(kv_rank, MLA_VW).T
    w_lat = jnp.concatenate([jnp.transpose(mla_w_uk, (1, 2, 0)),
                             jnp.zeros((MLA_HEADS, LANES - MLA_NOPE, kv_rank), F32)], axis=1)
    w_val = jnp.transpose(mla_w_uv, (1, 0, 2))
    bf = lambda a: a.astype(BF16)
    return dict(
        g0=norm_g[0:1], ng=norm_g, gq=mla_q_norm_g[None], gkv=mla_kv_norm_g[None],
        w_main=bf(w_main), w_g=bf(w_g), w_uq1=bf(w_uq1), w_uq2=bf(w_uq2), w_ukp=bf(w_ukp), w_uvt=bf(w_uvt),
        e_kpe=bf(e_kpe), w_lat=bf(w_lat), w_val=bf(w_val),
        w_a=bf(w_branch_a), w_b=bf(w_branch_b), w_o=bf(w_o), w_gu=bf(ffn_w_gu), w_dn=bf(ffn_w_down))


def _rope_tables(pos):
    freqs = ROPE_THETA ** (-jnp.arange(0, MLA_ROPE, 2, dtype=F32) / MLA_ROPE)
    ang = pos.astype(F32)[:, None] * freqs[None, :]
    cos, sin = jnp.cos(ang), jnp.sin(ang)
    n = pos.shape[0]
    one = jnp.ones((n, MLA_NOPE), F32)
    zero = lambda w: jnp.zeros((n, w), F32)
    pad = LANES - MLA_NOPE - MLA_ROPE
    cosq = jnp.concatenate([one, cos, cos, zero(pad)], axis=1) * MLA_SCALE
    sinq = jnp.concatenate([zero(MLA_NOPE), -sin, sin, zero(pad)], axis=1) * MLA_SCALE
    part = jnp.concatenate([-sin, sin], axis=1)
    tabk = jnp.concatenate([cos, cos, part, zero(MLA_ROPE), part], axis=1)
    return cosq, sinq, tabk


def _layer(l, x_prompt, x_sample, cache_k, cache_v, cache_c, cache_p, page_table, c_prompt, c_sample, rel_bias,
           ada_w, ada_b, da_lambda, da_subln_g, w):
    batch, seq, d = x_prompt.shape
    n_seq, dec_seq, _ = x_sample.shape
    assert dec_seq == 1
    lam_init = 0.8 - 0.6 * math.exp(-0.3 * l)
    n_pool, page = cache_k.shape[1:3]
    past = page_table.shape[1] * page
    subg = da_subln_g[None]

    mod = _adaln(jnp.concatenate([c_prompt, c_sample], axis=0), ada_w, ada_b)
    mods_p = [mod[:batch, i * d:(i + 1) * d].reshape(batch, 1, d) for i in range(6)]
    mods_s = [mod[batch:, i * d:(i + 1) * d].reshape(1, n_seq, d) for i in range(6)]

    xp = x_prompt.reshape(batch * seq, d)
    tm = TILE
    (qd, kd, vd, kdb, vdt, ckv, kpe, qm, km, vmt, gates) = _inproj(
        xp, mods_p[1], mods_p[0], w, _rope_tables(jnp.arange(seq, dtype=jnp.int32)), tm, seq // tm)
    xs = x_sample.reshape(n_seq, d)
    pos_s = jnp.full((n_seq,), past, jnp.int32)
    (qd_s, kd_s, vd_s, _, _, ckv_s, kpe_s, qm_s, _, _, gates_s) = _inproj(
        xs, mods_s[1], mods_s[0], w, _rope_tables(pos_s), n_seq, 1)
    kv_rank = ckv_s.shape[1]
    qlat = _headproj(qm_s, w["w_lat"], BF16, "latent_query").reshape(n_seq, MLA_HEADS, kv_rank)
    qpe = qm_s.reshape(n_seq, MLA_HEADS, LANES)[:, :, MLA_NOPE:MLA_NOPE + MLA_ROPE]
    pad_rows = lambda a, r: jnp.pad(a, ((0, 0), (0, r - a.shape[1]), (0, 0)))

    oa, ob, oa_s, olat = _attention(
        lam_init, rel_bias, da_lambda, subg, qd, qm, kdb, vdt, km, vmt, batch, seq, page_table,
        pad_rows(qd_s.reshape(n_seq, DA_HEADS, LANES), SUBLANES), pad_rows(qlat, ROWS), pad_rows(qpe, ROWS),
        pad_rows(kd_s, SUBLANES), pad_rows(vd_s, SUBLANES), ckv_s, kpe_s,
        cache_k.reshape(n_pool, page * DA_HEADS, LANES), cache_v.reshape(n_pool, page * DA_HEADS, LANES),
        cache_c.reshape(n_pool, page, kv_rank), jnp.swapaxes(cache_p.reshape(n_pool, page, MLA_ROPE), 1, 2))

    y_p = _post(xp, oa, ob, gates, [mods_p[2], mods_p[3], mods_p[4], mods_p[5]], w, tm, seq // tm)
    ob_s = _headproj(olat.reshape(n_seq, MLA_HEADS * kv_rank), w["w_val"], BF16, "value_up")
    y_s = _post(xs, oa_s.reshape(n_seq, DA_W).astype(BF16), ob_s, gates_s,
                [mods_s[2], mods_s[3], mods_s[4], mods_s[5]], w, n_seq, 1)
    state_p = (kd.reshape(batch, seq, DA_HEADS, 2 * DA_DH), vd.reshape(batch, seq, DA_HEADS, DA_VD),
               ckv.reshape(batch, seq, -1), kpe.reshape(batch, seq, MLA_ROPE))
    state_s = (kd_s.reshape(n_seq, 1, DA_HEADS, 2 * DA_DH), vd_s.reshape(n_seq, 1, DA_HEADS, DA_VD),
               ckv_s.reshape(n_seq, 1, -1), kpe_s.reshape(n_seq, 1, MLA_ROPE))
    return y_p.reshape(batch, seq, d), y_s.reshape(n_seq, 1, d), state_p, state_s


def kernel(x_prompt, x_sample, cache_da_k, cache_da_v, cache_mla_ckv, cache_mla_kpe, page_table, c_prompt, c_sample, rel_bias, ada_w, ada_b, norm_g, w_in, da_lambda, da_subln_g, mla_q_norm_g, mla_kv_norm_g, mla_w_uq, mla_w_uk, mla_w_uv, w_branch_a, w_branch_b, w_o, ffn_w_gu, ffn_w_down):
    depth = ada_w.shape[0]
    y_p, y_s = x_prompt, x_sample
    st_p, st_s = [], []
    for l in range(depth):
        w = _layer_weights(norm_g[l], w_in[l], mla_q_norm_g[l], mla_kv_norm_g[l], mla_w_uq[l], mla_w_uk[l],
                           mla_w_uv[l], w_branch_a[l], w_branch_b[l], w_o[l], ffn_w_gu[l], ffn_w_down[l])
        y_p, y_s, sp, ss = _layer(l, y_p, y_s, cache_da_k[l:l + 1], cache_da_v[l:l + 1], cache_mla_ckv[l:l + 1],
                                  cache_mla_kpe[l:l + 1], page_table, c_prompt, c_sample, rel_bias, ada_w[l],
                                  ada_b[l], da_lambda[l], da_subln_g[l], w)
        st_p.append(sp)
        st_s.append(ss)
    stack = lambda sts, i: jnp.stack([s[i] for s in sts])
    return (y_p, y_s, stack(st_p, 0), stack(st_p, 1), stack(st_p, 2), stack(st_p, 3),
            stack(st_s, 0), stack(st_s, 1), stack(st_s, 2), stack(st_s, 3))
```

```python
import functools
import math

import numpy as np
import jax
import jax.numpy as jnp
from jax import lax
from jax.experimental import pallas as pl
from jax.experimental.pallas import tpu as pltpu

F32 = jnp.float32
BF16 = jnp.bfloat16

DA_HEADS = 4
DA_DH = 64
DA_VD = 2 * DA_DH
MLA_HEADS = 8
MLA_NOPE = 64
MLA_ROPE = 32
MLA_VD = 64
ROPE_THETA = 10000.0
REL_BUCKETS = 32
REL_MAX_DIST = 128
EPS = 1e-6
LOG2E = math.log2(math.e)
DA_SCALE = DA_DH ** -0.5 * LOG2E
MLA_SCALE = (MLA_NOPE + MLA_ROPE) ** -0.5 * LOG2E

LANES = 128
SUBLANES = 8
DA_W = DA_HEADS * LANES
MLA_W = MLA_HEADS * LANES
MLA_VW = MLA_HEADS * MLA_VD
BIAS_CUTOFF = 113
NEG = -1e30
VMEM_LIMIT = 56 * 1024 * 1024

TILE = 256
N_MAPS = 2 * DA_HEADS + MLA_HEADS
CHUNK_PAGES = 16
SUB_PAGES = 4
CHUNK_SLOTS = 2
ROWS = 16

NT_DIMS = (((1,), (1,)), ((), ()))


def _rms(x, g):
    return x * lax.rsqrt(jnp.mean(x * x, axis=-1, keepdims=True) + EPS) * g


def _dot(a, b):
    return jnp.dot(a, b, preferred_element_type=F32)


def _dot_nt(a, b):
    return lax.dot_general(a, b, NT_DIMS, preferred_element_type=F32)


def _const_spec(shape):
    nd = len(shape)
    return pl.BlockSpec(shape, lambda *_: (0,) * nd)


def _adaln_kernel(c_ref, w_ref, b_ref, o_ref):
    c = c_ref[...]
    a = (c * jax.nn.sigmoid(c)).astype(BF16)
    o_ref[...] = _dot(a, w_ref[...].astype(BF16)) + b_ref[...]


def _adaln(c_all, ada_w, ada_b):
    r, d = c_all.shape
    n = ada_w.shape[1]
    tn = 768
    return pl.pallas_call(
        _adaln_kernel,
        grid=(n // tn,),
        in_specs=[pl.BlockSpec((r, d), lambda j: (0, 0)),
                  pl.BlockSpec((d, tn), lambda j: (0, j)),
                  pl.BlockSpec((1, tn), lambda j: (0, j))],
        out_specs=pl.BlockSpec((r, tn), lambda j: (0, j)),
        out_shape=jax.ShapeDtypeStruct((r, n), F32),
        compiler_params=pltpu.CompilerParams(dimension_semantics=("arbitrary",), vmem_limit_bytes=VMEM_LIMIT),
        name="adaln",
    )(c_all, ada_w, ada_b.reshape(1, n))


_C_QD, _C_KD, _C_VD = 0, DA_W, 2 * DA_W
_C_CQ = 3 * DA_W


def _inproj_kernel(q_rank, kv_rank,
                   x_ref, sc_ref, sh_ref, g0_ref, wmain_ref, wg_ref, gq_ref, gkv_ref,
                   wuq1_ref, wuq2_ref, wukp_ref, wuvt_ref, ekpe_ref, cosq_ref, sinq_ref, tabk_ref,
                   qd_ref, kd_ref, vd_ref, kdb_ref, vdt_ref, ckv_ref, kpe_ref, qm_ref, km_ref, vmt_ref, gt_ref):
    c_ckv = _C_CQ + q_rank
    c_kpe = c_ckv + kv_rank
    h = _rms(x_ref[...], g0_ref[...]) * (1.0 + sc_ref[0]) + sh_ref[0]
    hb = h.astype(BF16)

    def mm(lo, hi):
        return _dot(hb, wmain_ref[:, lo:hi])

    qd_ref[...] = (mm(_C_QD, _C_KD) * DA_SCALE).astype(BF16)
    kd = mm(_C_KD, _C_VD)
    vd = mm(_C_VD, _C_CQ)
    for hh in range(DA_HEADS):
        cols = slice(hh * LANES, (hh + 1) * LANES)
        kd_ref[:, hh, :] = kd[:, cols]
        vd_ref[:, hh, :] = vd[:, cols]
    kdb_ref[...] = kd.astype(BF16)
    vdt_ref[0] = vd.T.astype(BF16)

    cq = _rms(mm(_C_CQ, c_ckv), gq_ref[...]).astype(BF16)
    q1 = _dot(cq, wuq1_ref[...])
    q2 = _dot(cq, wuq2_ref[...])
    cosq = cosq_ref[...]
    sinq = sinq_ref[...]
    for hh in range(MLA_HEADS):
        sl = slice(hh * LANES, (hh + 1) * LANES)
        qm_ref[:, sl] = (q1[:, sl] * cosq + q2[:, sl] * sinq).astype(BF16)

    ckv = _rms(mm(c_ckv, c_kpe), gkv_ref[...])
    ckv_ref[...] = ckv
    cb = ckv.astype(BF16)
    r = mm(c_kpe, c_kpe + LANES) * tabk_ref[...]
    kr = r + pltpu.roll(r, 3 * LANES // 4, 1)
    kpe_ref[...] = kr[:, :MLA_ROPE]
    km_ref[...] = (_dot(cb, wukp_ref[...]) + _dot(kr.astype(BF16), ekpe_ref[...])).astype(BF16)
    vmt_ref[0] = _dot_nt(wuvt_ref[...], cb).astype(BF16)

    gt_ref[...] = jax.nn.sigmoid(_dot(hb, wg_ref[...])).astype(BF16)


def _inproj(x2d, sc, sh, w, tabs, tm, rows_per_mod_block):
    t, d = x2d.shape
    q_rank = w["gq"].shape[1]
    kv_rank = w["gkv"].shape[1]
    rmod = sc.shape[1]
    n_tab_blocks = tabs[0].shape[0] // tm
    n_tiles = t // tm
    row = lambda width: pl.BlockSpec((tm, width), lambda i: (i, 0))
    heads = pl.BlockSpec((tm, DA_HEADS, LANES), lambda i: (i, 0, 0))
    transposed = lambda width: pl.BlockSpec((1, width, tm), lambda i: (i, 0, 0))
    mod_spec = pl.BlockSpec((1, rmod, d), lambda i: (i // rows_per_mod_block, 0, 0))
    tab_spec = pl.BlockSpec((tm, LANES), lambda i: (i % n_tab_blocks, 0))
    weights = [w["g0"], w["w_main"], w["w_g"], w["gq"], w["gkv"], w["w_uq1"], w["w_uq2"], w["w_ukp"], w["w_uvt"],
               w["e_kpe"]]
    in_specs = ([row(d), mod_spec, mod_spec] + [_const_spec(a.shape) for a in weights] + [tab_spec] * 3)
    sds = jax.ShapeDtypeStruct
    outs = [
        (row(DA_W), sds((t, DA_W), BF16)),
        (heads, sds((t, DA_HEADS, LANES), F32)),
        (heads, sds((t, DA_HEADS, LANES), F32)),
        (row(DA_W), sds((t, DA_W), BF16)),
        (transposed(DA_W), sds((n_tiles, DA_W, tm), BF16)),
        (row(kv_rank), sds((t, kv_rank), F32)),
        (row(MLA_ROPE), sds((t, MLA_ROPE), F32)),
        (row(MLA_W), sds((t, MLA_W), BF16)),
        (row(MLA_W), sds((t, MLA_W), BF16)),
        (transposed(MLA_VW), sds((n_tiles, MLA_VW, tm), BF16)),
        (row(2 * d), sds((t, 2 * d), BF16)),
    ]
    return pl.pallas_call(
        functools.partial(_inproj_kernel, q_rank, kv_rank),
        grid=(n_tiles,),
        in_specs=in_specs,
        out_specs=[o[0] for o in outs],
        out_shape=[o[1] for o in outs],
        compiler_params=pltpu.CompilerParams(dimension_semantics=("arbitrary",), vmem_limit_bytes=VMEM_LIMIT),
        name="inproj",
    )(x2d, sc, sh, *weights, *tabs)


def _t5_bucket(dist):
    n = jnp.maximum(dist, 0)
    max_exact = REL_BUCKETS // 2
    nf = jnp.maximum(n, max_exact).astype(F32)
    large = max_exact + (jnp.log(nf / max_exact) / math.log(REL_MAX_DIST / max_exact)
                         * (REL_BUCKETS - max_exact)).astype(jnp.int32)
    large = jnp.minimum(large, REL_BUCKETS - 1)
    return jnp.where(n < max_exact, n, large)


def _shifted_bias(relb_ref, bucket, head):
    last = relb_ref[REL_BUCKETS - 1, head]
    out = jnp.zeros(bucket.shape, F32)
    for b in range(REL_BUCKETS - 1):
        out = jnp.where(bucket == b, (relb_ref[b, head] - last) * LOG2E, out)
    return out


def _lambda_full(lam_ref, lam_init):
    lf = lam_ref[...]
    a = jnp.sum(lf[0:1] * lf[1:2], axis=1, keepdims=True)
    b = jnp.sum(lf[2:3] * lf[3:4], axis=1, keepdims=True)
    return jnp.exp(a) - jnp.exp(b) + lam_init


def _attn_kernel(lam_init, past, page, n_groups, n_chunks, chunks_per_step,
                 pt_ref,
                 relb_ref, lam_ref, subg_ref, qd_ref, qm_ref, kd_ref, vdt_ref, km_ref, vmt_ref,
                 q8_ref, qlat_ref, qpe_ref, knew_ref, vnew_ref, cnew_ref, pnew_ref,
                 ck_hbm, cv_hbm, cc_hbm, cp_hbm,
                 oa_ref, ob_ref, oas_ref, olat_ref,
                 bias_ref, qh_ref, m_ref, l_ref, acc_da_ref, acc_ml_ref,
                 kbuf, vbuf, cbuf, pbuf, sem, mda_ref, lda_ref, ada_ref, mml_ref, lml_ref, aml_ref, near_ref):
    t = TILE
    bi = pl.program_id(0)
    qi = pl.program_id(1)
    nq = pl.num_programs(1)
    n_da = 2 * DA_HEADS
    npg = CHUNK_PAGES
    lam = _lambda_full(lam_ref, lam_init)
    rows_pair = page * DA_HEADS // 2
    row1 = lax.broadcasted_iota(jnp.int32, (ROWS, LANES), 0)
    lane1 = lax.broadcasted_iota(jnp.int32, (ROWS, LANES), 1)
    rowc = lax.broadcasted_iota(jnp.int32, (ROWS, rows_pair), 0)
    colc = lax.broadcasted_iota(jnp.int32, (ROWS, rows_pair), 1)
    own = (colc & 1) == lax.shift_right_logical(rowc & (SUBLANES - 1), 1)

    def chunk_copies(c, slot):
        seq = lax.div(c, n_groups)
        first = lax.rem(c, n_groups) * npg
        copies = []
        for p in range(npg):
            pg = pt_ref[seq, first + p]
            copies += [pltpu.make_async_copy(ck_hbm.at[pg], kbuf.at[slot, p], sem.at[slot]),
                       pltpu.make_async_copy(cv_hbm.at[pg], vbuf.at[slot, p], sem.at[slot]),
                       pltpu.make_async_copy(cc_hbm.at[pg], cbuf.at[slot, p], sem.at[slot]),
                       pltpu.make_async_copy(cp_hbm.at[pg], pbuf.at[slot, p], sem.at[slot])]
        return copies

    def head_bias(bucket):
        out = jnp.zeros(bucket.shape, F32)
        r = lax.broadcasted_iota(jnp.int32, bucket.shape, 0) & (SUBLANES - 1)
        for hh in range(DA_HEADS):
            out = jnp.where(r == hh, _shifted_bias(relb_ref, bucket, hh), out)
        return out

    def softmax_update(s_list, mx_ref, sum_ref, a_ref, values):
        m_old = mx_ref[...]
        m_new = m_old
        for s in s_list:
            m_new = jnp.maximum(m_new, jnp.max(s, axis=1, keepdims=True))
        alpha = jnp.exp2(m_old - m_new)
        l_new = alpha * sum_ref[...]
        acc = alpha * a_ref[...]
        for s, v in zip(s_list, values):
            p = jnp.exp2(s - m_new)
            l_new = l_new + jnp.sum(p, axis=1, keepdims=True)
            acc = acc + _dot(p.astype(BF16), v)
        mx_ref[...] = m_new
        sum_ref[...] = l_new
        a_ref[...] = acc

    def chunk_begin(c):
        slot = lax.rem(c, CHUNK_SLOTS)
        for cp in chunk_copies(c, slot):
            cp.wait()
        seq = lax.div(c, n_groups)
        grp = lax.rem(c, n_groups)
        q8 = q8_ref[seq].astype(F32)
        q16 = jnp.concatenate([q8, q8], axis=0)
        qhalf = jnp.where(lax.shift_right_logical(lane1, 6) == lax.shift_right_logical(row1, 3), q16, 0.0)
        qrows = jnp.concatenate([jnp.where((row1 & 1) == 0, qhalf, 0.0), jnp.where((row1 & 1) == 1, qhalf, 0.0)],
                                axis=1).astype(BF16)
        qlat = qlat_ref[seq]
        qpe = qpe_ref[seq]

        @pl.when(grp == 0)
        def _():
            kn = knew_ref[seq].astype(BF16).astype(F32)
            kn = jnp.concatenate([kn, kn], axis=0)
            mda_ref[...] = (jnp.sum(qhalf.astype(BF16).astype(F32) * kn, axis=1, keepdims=True)
                            + head_bias(jnp.zeros((ROWS, 1), jnp.int32)))
            lda_ref[...] = jnp.ones((ROWS, 1), F32)
            vn = vnew_ref[seq].astype(BF16).astype(F32)
            vn = jnp.concatenate([vn, vn], axis=0)
            ada_ref[...] = jnp.concatenate([vn, vn], axis=1)
            cn = cnew_ref[pl.ds(seq, 1), :].astype(BF16).astype(F32)
            pn = pnew_ref[pl.ds(seq, 1), :].astype(BF16).astype(F32)
            mml_ref[...] = (jnp.sum(qlat.astype(F32) * cn, axis=1, keepdims=True)
                            + jnp.sum(qpe.astype(F32) * pn, axis=1, keepdims=True))
            lml_ref[...] = jnp.ones((ROWS, 1), F32)
            aml_ref[...] = jnp.broadcast_to(cn, aml_ref.shape)

        return dict(c=c, slot=slot, seq=seq, grp=grp, qrows=qrows, qlat=qlat, qpe=qpe)

    def paired(buf, slot, i):
        even = buf[slot, i, pl.ds(0, rows_pair, stride=2), :]
        odd = buf[slot, i, pl.ds(1, rows_pair, stride=2), :]
        return jnp.concatenate([even, odd], axis=1).astype(BF16)

    def chunk_page_scores(ch, i):
        slot = ch["slot"]
        cb = cbuf[slot, i].astype(BF16)
        s_da = jnp.where(own, _dot_nt(ch["qrows"], paired(kbuf, slot, i)), NEG)
        if i == npg - 1:
            near = near_ref[...]
            s_da = s_da + jnp.where(ch["grp"] == n_groups - 1, near, jnp.zeros_like(near))
        ch.setdefault("cb", {})[i] = cb
        ch.setdefault("s_da", {})[i] = s_da
        ch.setdefault("s_ml", {})[i] = _dot_nt(ch["qlat"], cb) + _dot(ch["qpe"], pbuf[slot, i].astype(BF16))

    def chunk_sub_update(ch, k):
        slot = ch["slot"]
        pages = range(k * SUB_PAGES, (k + 1) * SUB_PAGES)
        softmax_update([ch["s_da"][i] for i in pages], mda_ref, lda_ref, ada_ref,
                       [paired(vbuf, slot, i) for i in pages])
        softmax_update([ch["s_ml"][i] for i in pages], mml_ref, lml_ref, aml_ref, [ch["cb"][i] for i in pages])
        if k == npg // SUB_PAGES - 1:
            for cp in chunk_copies(jnp.minimum(ch["c"] + CHUNK_SLOTS, n_chunks - 1), slot):
                cp.start()

    def chunk_end(ch):
        @pl.when(ch["grp"] == n_groups - 1)
        def _():
            accn = ada_ref[...] * (1.0 / lda_ref[...])
            accn = jnp.where((row1 & 1) == 0, accn[:, :LANES], accn[:, LANES:])
            o = accn[:DA_HEADS] - lam * accn[SUBLANES:SUBLANES + DA_HEADS]
            oas_ref[ch["seq"]] = _rms(o, subg_ref[...]) * (1.0 - lam_init)
            olat_ref[ch["seq"]] = (aml_ref[...] * (1.0 / lml_ref[...]))[:MLA_HEADS]

    @pl.when((bi == 0) & (qi == 0))
    def _():
        for c in range(min(CHUNK_SLOTS, n_chunks)):
            for cp in chunk_copies(c, c):
                cp.start()
        last_page_pos = (n_groups * npg - 1) * page + lax.shift_right_logical(colc, 1)
        near_ref[...] = head_bias(_t5_bucket(past - last_page_pos))

    @pl.when((bi == 0) & (qi == 0))
    def _():
        key = lax.broadcasted_iota(jnp.int32, (t, t), 0)
        qry = lax.broadcasted_iota(jnp.int32, (t, t), 1)
        d0 = qry - key
        b0 = _t5_bucket(d0)
        b1 = _t5_bucket(d0 + t)
        for hh in range(DA_HEADS):
            bias_ref[hh] = jnp.where(d0 >= 0, _shifted_bias(relb_ref, b0, hh), NEG)
            bias_ref[DA_HEADS + hh] = _shifted_bias(relb_ref, b1, hh)
        bias_ref[2 * DA_HEADS] = jnp.where(d0 >= 0, 0.0, NEG)

    lane = lax.broadcasted_iota(jnp.int32, (t, LANES), 1)
    for hh in range(DA_HEADS):
        q = qd_ref[:, hh * LANES:(hh + 1) * LANES].astype(F32)
        qh_ref[2 * hh] = jnp.where(lane < DA_DH, q, 0.0).astype(BF16)
        qh_ref[2 * hh + 1] = jnp.where(lane >= DA_DH, q, 0.0).astype(BF16)

    m_ref[...] = jnp.full(m_ref.shape, NEG, F32)
    l_ref[...] = jnp.zeros(l_ref.shape, F32)
    acc_da_ref[...] = jnp.zeros(acc_da_ref.shape, F32)
    acc_ml_ref[...] = jnp.zeros(acc_ml_ref.shape, F32)

    first_step = bi * ((nq * (nq + 1)) // 2) + (qi * (qi + 1)) // 2

    def step(j, kind):
        n = first_step + j

        @pl.when((n + 1) * chunks_per_step <= n_chunks)
        def _():
            flash_step(j, kind, [n * chunks_per_step + u for u in range(chunks_per_step)])

        @pl.when(n * chunks_per_step >= n_chunks)
        def _():
            flash_step(j, kind, [])

    def flash_step(j, kind, chunk_ids):
        chunks = [chunk_begin(c) for c in chunk_ids]
        page_jobs = [(ch, p) for ch in chunks for p in range(npg)]
        update_jobs = [(ch, k) for ch in chunks for k in range(npg // SUB_PAGES)]

        def deal(jobs, i):
            per_map = -(-len(jobs) // N_MAPS)
            return jobs[i * per_map:(i + 1) * per_map]

        scores = []
        for i in range(N_MAPS):
            if i < n_da:
                hh = i // 2
                k = kd_ref[0, j, :, hh * LANES:(hh + 1) * LANES]
                q = qh_ref[i]
                bias = {"far": None, "near": DA_HEADS + hh, "diag": hh}[kind]
            else:
                hh = i - n_da
                k = km_ref[0, j, :, hh * LANES:(hh + 1) * LANES]
                q = qm_ref[:, hh * LANES:(hh + 1) * LANES]
                bias = {"far": None, "near": None, "diag": 2 * DA_HEADS}[kind]
            s = _dot_nt(k, q)
            if bias is not None:
                s = s + bias_ref[bias]
            scores.append(s)
            for ch, p in deal(page_jobs, i):
                chunk_page_scores(ch, p)
        probs = []
        for i in range(N_MAPS):
            s = scores[i]
            m_old = m_ref[i]
            m_new = jnp.maximum(m_old, jnp.max(s, axis=0, keepdims=True))
            alpha = jnp.exp2(m_old - m_new)
            p = jnp.exp2(s - m_new)
            l_ref[i] = alpha * l_ref[i] + jnp.sum(p, axis=0, keepdims=True)
            m_ref[i] = m_new
            probs.append((alpha, p.astype(BF16)))
        maps_per_update = max(1, N_MAPS // max(1, len(update_jobs)))
        for i in range(N_MAPS):
            if i % maps_per_update == 0 and i // maps_per_update < len(update_jobs):
                chunk_sub_update(*update_jobs[i // maps_per_update])
            alpha, p = probs[i]
            if i < n_da:
                hh = i // 2
                vt = vdt_ref[j, hh * LANES:(hh + 1) * LANES, :]
                acc_ref, ai = acc_da_ref, i
            else:
                hh = i - n_da
                vt = vmt_ref[j, hh * MLA_VD:(hh + 1) * MLA_VD, :]
                acc_ref, ai = acc_ml_ref, hh
            acc_ref[ai] = alpha * acc_ref[ai] + _dot(vt, p)
        for job in update_jobs[-(-N_MAPS // maps_per_update):]:
            chunk_sub_update(*job)
        for ch in chunks:
            chunk_end(ch)

    def far(j, carry):
        step(j, "far")
        return carry

    lax.fori_loop(0, qi - 1, far, 0)

    @pl.when(qi >= 1)
    def _():
        step(qi - 1, "near")

    step(qi, "diag")

    @pl.when((bi == pl.num_programs(0) - 1) & (qi == nq - 1))
    def _():
        for slot in range(min(CHUNK_SLOTS, n_chunks)):
            for cp in chunk_copies(n_chunks - 1, slot):
                cp.wait()

    for hh in range(DA_HEADS):
        ot = (acc_da_ref[2 * hh] * (1.0 / l_ref[2 * hh])
              - lam * (acc_da_ref[2 * hh + 1] * (1.0 / l_ref[2 * hh + 1])))
        oa_ref[:, hh * LANES:(hh + 1) * LANES] = (_rms(ot.T, subg_ref[...]) * (1.0 - lam_init)).astype(BF16)
    for pair in range(MLA_HEADS // 2):
        ha, hb = 2 * pair, 2 * pair + 1
        ot = jnp.concatenate([acc_ml_ref[ha] * (1.0 / l_ref[n_da + ha]),
                              acc_ml_ref[hb] * (1.0 / l_ref[n_da + hb])], axis=0)
        ob_ref[:, pair * LANES:(pair + 1) * LANES] = ot.T.astype(BF16)


def _attention(lam_init, rel_bias, da_lambda, subg, qd, qm, kdb, vdt, km, vmt, batch, seq,
               page_table, q8, qlat, qpe, knew, vnew, cnew, pnew, cache_k, cache_v, cache_c, cache_pt):
    t = TILE
    nq = seq // t
    n_seq, n_pages = page_table.shape
    page = cache_c.shape[1]
    kv_rank = cache_c.shape[2]
    npg = CHUNK_PAGES
    assert n_pages % npg == 0 and page >= BIAS_CUTOFF
    n_groups = n_pages // npg
    n_chunks = n_seq * n_groups
    n_steps = batch * (nq * (nq + 1)) // 2
    chunks_per_step = -(-n_chunks // n_steps)
    assert n_groups % chunks_per_step == 0
    past = n_pages * page

    tiles = lambda a: a.reshape(batch, nq, t, a.shape[-1])
    once = pl.Buffered(1)
    kv_spec = lambda width: pl.BlockSpec((1, nq, t, width), lambda b, i, pt: (b, 0, 0, 0), pipeline_mode=once)
    vt_spec = lambda width: pl.BlockSpec((nq, width, t), lambda b, i, pt: (b, 0, 0), pipeline_mode=once)
    q_spec = lambda width: pl.BlockSpec((t, width), lambda b, i, pt: (b * nq + i, 0))
    whole = lambda a: pl.BlockSpec(a.shape, lambda b, i, pt: (0,) * a.ndim, pipeline_mode=once)
    hbm = pl.BlockSpec(memory_space=pl.ANY)
    sds = jax.ShapeDtypeStruct
    out_shape = [sds((batch * seq, DA_W), BF16), sds((batch * seq, MLA_VW), BF16),
                 sds((n_seq, DA_HEADS, LANES), F32), sds((n_seq, MLA_HEADS, kv_rank), F32)]
    grid_spec = pltpu.PrefetchScalarGridSpec(
        num_scalar_prefetch=1,
        grid=(batch, nq),
        in_specs=[pl.BlockSpec(memory_space=pltpu.SMEM), whole(da_lambda), whole(subg),
                  q_spec(DA_W), q_spec(MLA_W), kv_spec(DA_W), vt_spec(DA_W), kv_spec(MLA_W), vt_spec(MLA_VW),
                  whole(q8), whole(qlat), whole(qpe), whole(knew), whole(vnew), whole(cnew), whole(pnew),
                  hbm, hbm, hbm, hbm],
        out_specs=[q_spec(DA_W), q_spec(MLA_VW),
                   pl.BlockSpec(out_shape[2].shape, lambda b, i, pt: (0, 0, 0)),
                   pl.BlockSpec(out_shape[3].shape, lambda b, i, pt: (0, 0, 0))],
        scratch_shapes=[pltpu.VMEM((2 * DA_HEADS + 1, t, t), F32), pltpu.VMEM((2 * DA_HEADS, t, LANES), BF16),
                        pltpu.VMEM((N_MAPS, 1, t), F32), pltpu.VMEM((N_MAPS, 1, t), F32),
                        pltpu.VMEM((2 * DA_HEADS, LANES, t), F32), pltpu.VMEM((MLA_HEADS, MLA_VD, t), F32),
                        pltpu.VMEM((CHUNK_SLOTS, npg, page * DA_HEADS, LANES), F32),
                        pltpu.VMEM((CHUNK_SLOTS, npg, page * DA_HEADS, LANES), F32),
                        pltpu.VMEM((CHUNK_SLOTS, npg, page, kv_rank), F32),
                        pltpu.VMEM((CHUNK_SLOTS, npg, MLA_ROPE, page), F32),
                        pltpu.SemaphoreType.DMA((CHUNK_SLOTS,)),
                        pltpu.VMEM((ROWS, 1), F32), pltpu.VMEM((ROWS, 1), F32), pltpu.VMEM((ROWS, 2 * LANES), F32),
                        pltpu.VMEM((ROWS, 1), F32), pltpu.VMEM((ROWS, 1), F32), pltpu.VMEM((ROWS, kv_rank), F32),
                        pltpu.VMEM((ROWS, page * DA_HEADS // 2), F32)],
    )
    return pl.pallas_call(
        functools.partial(_attn_kernel, lam_init, past, page, n_groups, n_chunks, chunks_per_step),
        grid_spec=grid_spec,
        out_shape=out_shape,
        compiler_params=pltpu.CompilerParams(dimension_semantics=("arbitrary", "arbitrary"),
                                             vmem_limit_bytes=VMEM_LIMIT),
        name="attention",
    )(page_table, rel_bias, da_lambda, subg, qd, qm, tiles(kdb), vdt, tiles(km), vmt,
      q8, qlat, qpe, knew, vnew, cnew, pnew, cache_k, cache_v, cache_c, cache_pt)


def _headproj_kernel(n_heads, in_w, out_w, x_ref, w_ref, o_ref):
    for hh in range(n_heads):
        x = x_ref[:, hh * in_w:(hh + 1) * in_w].astype(BF16)
        o_ref[:, hh * out_w:(hh + 1) * out_w] = _dot(x, w_ref[hh]).astype(o_ref.dtype)


def _headproj(x, w, out_dtype, name):
    n_heads, in_w, out_w = w.shape
    r = x.shape[0]
    return pl.pallas_call(
        functools.partial(_headproj_kernel, n_heads, in_w, out_w),
        out_shape=jax.ShapeDtypeStruct((r, n_heads * out_w), out_dtype),
        compiler_params=pltpu.CompilerParams(vmem_limit_bytes=VMEM_LIMIT),
        name=name,
    )(x, w)


def _post_kernel(d_ff, x_ref, oa_ref, ob_ref, gt_ref, gt1_ref, sh2_ref, sc2_ref, gt2_ref, ng_ref,
                 wa_ref, wb_ref, wo_ref, wgu_ref, wdn_ref, y_ref):
    d = x_ref.shape[1]
    x = x_ref[...]
    gates = gt_ref[...]
    merged = (gates[:, :d].astype(F32) * _dot(oa_ref[...], wa_ref[...])
              + gates[:, d:].astype(F32) * _dot(ob_ref[...], wb_ref[...]))
    x1 = x + gt1_ref[0] * _rms(_dot(merged.astype(BF16), wo_ref[...]), ng_ref[1:2])
    h2 = (_rms(x1, ng_ref[2:3]) * (1.0 + sc2_ref[0]) + sh2_ref[0]).astype(BF16)
    gg = _dot(h2, wgu_ref[:, :d_ff])
    uu = _dot(h2, wgu_ref[:, d_ff:])
    act = (gg * jax.nn.sigmoid(gg) * uu).astype(BF16)
    y_ref[...] = x1 + gt2_ref[0] * _rms(_dot(act, wdn_ref[...]), ng_ref[3:4])


def _post(x2d, oa, ob, gates, mods, w, tm, rows_per_mod_block):
    t, d = x2d.shape
    d_ff = w["w_dn"].shape[0]
    rmod = mods[0].shape[1]
    row = lambda width: pl.BlockSpec((tm, width), lambda i: (i, 0))
    mod_spec = pl.BlockSpec((1, rmod, d), lambda i: (i // rows_per_mod_block, 0, 0))
    weights = [w["ng"], w["w_a"], w["w_b"], w["w_o"], w["w_gu"], w["w_dn"]]
    return pl.pallas_call(
        functools.partial(_post_kernel, d_ff),
        grid=(t // tm,),
        in_specs=[row(d), row(oa.shape[1]), row(ob.shape[1]), row(2 * d)] + [mod_spec] * 4
                 + [_const_spec(a.shape) for a in weights],
        out_specs=row(d),
        out_shape=jax.ShapeDtypeStruct((t, d), F32),
        compiler_params=pltpu.CompilerParams(dimension_semantics=("arbitrary",), vmem_limit_bytes=VMEM_LIMIT),
        name="post",
    )(x2d, oa, ob, gates, *mods, *weights)


def _layer_weights(norm_g, w_in, mla_q_norm_g, mla_kv_norm_g, mla_w_uq, mla_w_uk, mla_w_uv, w_branch_a,
                   w_branch_b, w_o, ffn_w_gu, ffn_w_down):
    d = w_in.shape[0]
    q_rank = mla_q_norm_g.shape[0]
    kv_rank = mla_kv_norm_g.shape[0]
    half = MLA_ROPE // 2
    c0 = 3 * DA_W + q_rank + kv_rank
    w_kpe = w_in[:, c0:c0 + MLA_ROPE]
    partner = jnp.concatenate([w_kpe[:, half:], w_kpe[:, :half]], axis=1)
    w_main = jnp.concatenate([w_in[:, :c0], w_kpe, partner, jnp.zeros((d, MLA_ROPE), F32), partner], axis=1)
    w_g = w_in[:, c0 + MLA_ROPE:]

    pad = LANES - MLA_NOPE - MLA_ROPE
    zq = lambda n: jnp.zeros((q_rank, MLA_HEADS, n), F32)
    nope, r1, r2 = mla_w_uq[..., :MLA_NOPE], mla_w_uq[..., MLA_NOPE:MLA_NOPE + half], mla_w_uq[..., MLA_NOPE + half:]
    w_uq1 = jnp.concatenate([nope, r1, r2, zq(pad)], axis=-1).reshape(q_rank, MLA_W)
    w_uq2 = jnp.concatenate([zq(MLA_NOPE), r2, r1, zq(pad)], axis=-1).reshape(q_rank, MLA_W)
    zk = lambda n: jnp.zeros((kv_rank, MLA_HEADS, n), F32)
    w_ukp = jnp.concatenate([mla_w_uk, zk(LANES - MLA_NOPE)], axis=-1).reshape(kv_rank, MLA_W)
    e = np.zeros((LANES, MLA_HEADS, LANES), np.float32)
    for l in range(MLA_ROPE):
        e[l, :, MLA_NOPE + l] = 1.0
    e_kpe = jnp.asarray(e.reshape(LANES, MLA_W))
    w_uvt = mla_w_uv.reshape(kv_rank, MLA_VW).T
    w_lat = jnp.concatenate([jnp.transpose(mla_w_uk, (1, 2, 0)),
                             jnp.zeros((MLA_HEADS, LANES - MLA_NOPE, kv_rank), F32)], axis=1)
    w_val = jnp.transpose(mla_w_uv, (1, 0, 2))
    bf = lambda a: a.astype(BF16)
    return dict(
        g0=norm_g[0:1], ng=norm_g, gq=mla_q_norm_g[None], gkv=mla_kv_norm_g[None],
        w_main=bf(w_main), w_g=bf(w_g), w_uq1=bf(w_uq1), w_uq2=bf(w_uq2), w_ukp=bf(w_ukp), w_uvt=bf(w_uvt),
        e_kpe=bf(e_kpe), w_lat=bf(w_lat), w_val=bf(w_val),
        w_a=bf(w_branch_a), w_b=bf(w_branch_b), w_o=bf(w_o), w_gu=bf(ffn_w_gu), w_dn=bf(ffn_w_down))


def _rope_tables(pos):
    freqs = ROPE_THETA ** (-jnp.arange(0, MLA_ROPE, 2, dtype=F32) / MLA_ROPE)
    ang = pos.astype(F32)[:, None] * freqs[None, :]
    cos, sin = jnp.cos(ang), jnp.sin(ang)
    n = pos.shape[0]
    one = jnp.ones((n, MLA_NOPE), F32)
    zero = lambda w: jnp.zeros((n, w), F32)
    pad = LANES - MLA_NOPE - MLA_ROPE
    cosq = jnp.concatenate([one, cos, cos, zero(pad)], axis=1) * MLA_SCALE
    sinq = jnp.concatenate([zero(MLA_NOPE), -sin, sin, zero(pad)], axis=1) * MLA_SCALE
    part = jnp.concatenate([-sin, sin], axis=1)
    tabk = jnp.concatenate([cos, cos, part, zero(MLA_ROPE), part], axis=1)
    return cosq, sinq, tabk


def _layer(l, x_prompt, x_sample, cache_k, cache_v, cache_c, cache_p, page_table, c_prompt, c_sample, rel_bias,
           ada_w, ada_b, da_lambda, da_subln_g, w):
    batch, seq, d = x_prompt.shape
    n_seq, dec_seq, _ = x_sample.shape
    assert dec_seq == 1
    lam_init = 0.8 - 0.6 * math.exp(-0.3 * l)
    n_pool, page = cache_k.shape[1:3]
    past = page_table.shape[1] * page
    subg = da_subln_g[None]

    mod = _adaln(jnp.concatenate([c_prompt, c_sample], axis=0), ada_w, ada_b)
    mods_p = [mod[:batch, i * d:(i + 1) * d].reshape(batch, 1, d) for i in range(6)]
    mods_s = [mod[batch:, i * d:(i + 1) * d].reshape(1, n_seq, d) for i in range(6)]

    xp = x_prompt.reshape(batch * seq, d)
    tm = TILE
    (qd, kd, vd, kdb, vdt, ckv, kpe, qm, km, vmt, gates) = _inproj(
        xp, mods_p[1], mods_p[0], w, _rope_tables(jnp.arange(seq, dtype=jnp.int32)), tm, seq // tm)
    xs = x_sample.reshape(n_seq, d)
    pos_s = jnp.full((n_seq,), past, jnp.int32)
    (qd_s, kd_s, vd_s, _, _, ckv_s, kpe_s, qm_s, _, _, gates_s) = _inproj(
        xs, mods_s[1], mods_s[0], w, _rope_tables(pos_s), n_seq, 1)
    kv_rank = ckv_s.shape[1]
    qlat = _headproj(qm_s, w["w_lat"], BF16, "latent_query").reshape(n_seq, MLA_HEADS, kv_rank)
    qpe = qm_s.reshape(n_seq, MLA_HEADS, LANES)[:, :, MLA_NOPE:MLA_NOPE + MLA_ROPE]
    pad_rows = lambda a, r: jnp.pad(a, ((0, 0), (0, r - a.shape[1]), (0, 0)))

    oa, ob, oa_s, olat = _attention(
        lam_init, rel_bias, da_lambda, subg, qd, qm, kdb, vdt, km, vmt, batch, seq, page_table,
        pad_rows(qd_s.reshape(n_seq, DA_HEADS, LANES), SUBLANES), pad_rows(qlat, ROWS), pad_rows(qpe, ROWS),
        pad_rows(kd_s, SUBLANES), pad_rows(vd_s, SUBLANES), ckv_s, kpe_s,
        cache_k.reshape(n_pool, page * DA_HEADS, LANES), cache_v.reshape(n_pool, page * DA_HEADS, LANES),
        cache_c.reshape(n_pool, page, kv_rank), jnp.swapaxes(cache_p.reshape(n_pool, page, MLA_ROPE), 1, 2))

    y_p = _post(xp, oa, ob, gates, [mods_p[2], mods_p[3], mods_p[4], mods_p[5]], w, tm, seq // tm)
    ob_s = _headproj(olat.reshape(n_seq, MLA_HEADS * kv_rank), w["w_val"], BF16, "value_up")
    y_s = _post(xs, oa_s.reshape(n_seq, DA_W).astype(BF16), ob_s, gates_s,
                [mods_s[2], mods_s[3], mods_s[4], mods_s[5]], w, n_seq, 1)
    state_p = (kd.reshape(batch, seq, DA_HEADS, 2 * DA_DH), vd.reshape(batch, seq, DA_HEADS, DA_VD),
               ckv.reshape(batch, seq, -1), kpe.reshape(batch, seq, MLA_ROPE))
    state_s = (kd_s.reshape(n_seq, 1, DA_HEADS, 2 * DA_DH), vd_s.reshape(n_seq, 1, DA_HEADS, DA_VD),
               ckv_s.reshape(n_seq, 1, -1), kpe_s.reshape(n_seq, 1, MLA_ROPE))
    return y_p.reshape(batch, seq, d), y_s.reshape(n_seq, 1, d), state_p, state_s


def kernel(x_prompt, x_sample, cache_da_k, cache_da_v, cache_mla_ckv, cache_mla_kpe, page_table, c_prompt, c_sample, rel_bias, ada_w, ada_b, norm_g, w_in, da_lambda, da_subln_g, mla_q_norm_g, mla_kv_norm_g, mla_w_uq, mla_w_uk, mla_w_uv, w_branch_a, w_branch_b, w_o, ffn_w_gu, ffn_w_down):
    depth = ada_w.shape[0]
    y_p, y_s = x_prompt, x_sample
    st_p, st_s = [], []
    for l in range(depth):
        w = _layer_weights(norm_g[l], w_in[l], mla_q_norm_g[l], mla_kv_norm_g[l], mla_w_uq[l], mla_w_uk[l],
                           mla_w_uv[l], w_branch_a[l], w_branch_b[l], w_o[l], ffn_w_gu[l], ffn_w_down[l])
        y_p, y_s, sp, ss = _layer(l, y_p, y_s, cache_da_k[l:l + 1], cache_da_v[l:l + 1], cache_mla_ckv[l:l + 1],
                                  cache_mla_kpe[l:l + 1], page_table, c_prompt, c_sample, rel_bias, ada_w[l],
                                  ada_b[l], da_lambda[l], da_subln_g[l], w)
        st_p.append(sp)
        st_s.append(ss)
    stack = lambda sts, i: jnp.stack([s[i] for s in sts])
    return (y_p, y_s, stack(st_p, 0), stack(st_p, 1), stack(st_p, 2), stack(st_p, 3),
            stack(st_s, 0), stack(st_s, 1), stack(st_s, 2), stack(st_s, 3))
```

```python
import functools
import math

import numpy as np
import jax
import jax.numpy as jnp
from jax import lax
from jax.experimental import pallas as pl
from jax.experimental.pallas import tpu as pltpu

F32 = jnp.float32
BF16 = jnp.bfloat16

DA_HEADS = 4
DA_DH = 64
DA_VD = 2 * DA_DH
MLA_HEADS = 8
MLA_NOPE = 64
MLA_ROPE = 32
MLA_VD = 64
ROPE_THETA = 10000.0
REL_BUCKETS = 32
REL_MAX_DIST = 128
EPS = 1e-6
LOG2E = math.log2(math.e)
DA_SCALE = DA_DH ** -0.5 * LOG2E
MLA_SCALE = (MLA_NOPE + MLA_ROPE) ** -0.5 * LOG2E

LANES = 128
SUBLANES = 8
DA_W = DA_HEADS * LANES
MLA_W = MLA_HEADS * LANES
MLA_VW = MLA_HEADS * MLA_VD
BIAS_CUTOFF = 113
NEG = -1e30
VMEM_LIMIT = 56 * 1024 * 1024

TILE = 256
ROW_TILE = 512
N_MAPS = 2 * DA_HEADS + MLA_HEADS
CHUNK_PAGES = 16
SUB_PAGES = 4
CHUNK_SLOTS = 2
ROWS = 16

NT_DIMS = (((1,), (1,)), ((), ()))


def _rms(x, g):
    return x * lax.rsqrt(jnp.mean(x * x, axis=-1, keepdims=True) + EPS) * g


def _dot(a, b):
    return jnp.dot(a, b, preferred_element_type=F32)


def _dot_nt(a, b):
    return lax.dot_general(a, b, NT_DIMS, preferred_element_type=F32)


def _const_spec(shape):
    nd = len(shape)
    return pl.BlockSpec(shape, lambda *_: (0,) * nd)


def _adaln_kernel(c_ref, w_ref, b_ref, o_ref):
    c = c_ref[...]
    a = (c * jax.nn.sigmoid(c)).astype(BF16)
    o_ref[...] = _dot(a, w_ref[...].astype(BF16)) + b_ref[...]


def _adaln(c_all, ada_w, ada_b):
    r, d = c_all.shape
    n = ada_w.shape[1]
    tn = 768
    return pl.pallas_call(
        _adaln_kernel,
        grid=(n // tn,),
        in_specs=[pl.BlockSpec((r, d), lambda j: (0, 0)),
                  pl.BlockSpec((d, tn), lambda j: (0, j)),
                  pl.BlockSpec((1, tn), lambda j: (0, j))],
        out_specs=pl.BlockSpec((r, tn), lambda j: (0, j)),
        out_shape=jax.ShapeDtypeStruct((r, n), F32),
        compiler_params=pltpu.CompilerParams(dimension_semantics=("arbitrary",), vmem_limit_bytes=VMEM_LIMIT),
        name="adaln",
    )(c_all, ada_w, ada_b.reshape(1, n))


_C_QD, _C_KD, _C_VD = 0, DA_W, 2 * DA_W
_C_CQ = 3 * DA_W


def _inproj_kernel(q_rank, kv_rank,
                   x_ref, sc_ref, sh_ref, g0_ref, wmain_ref, wg_ref, gq_ref, gkv_ref,
                   wuq1_ref, wuq2_ref, wukp_ref, wuvt_ref, ekpe_ref, cosq_ref, sinq_ref, tabk_ref,
                   qd_ref, kd_ref, vd_ref, kdb_ref, vdt_ref, ckv_ref, kpe_ref, qm_ref, km_ref, vmt_ref, gt_ref):
    c_ckv = _C_CQ + q_rank
    c_kpe = c_ckv + kv_rank
    h = _rms(x_ref[...], g0_ref[...]) * (1.0 + sc_ref[0]) + sh_ref[0]
    hb = h.astype(BF16)

    def mm(lo, hi):
        return _dot(hb, wmain_ref[:, lo:hi])

    qd_ref[...] = (mm(_C_QD, _C_KD) * DA_SCALE).astype(BF16)
    kd = mm(_C_KD, _C_VD)
    vd = mm(_C_VD, _C_CQ)
    for hh in range(DA_HEADS):
        cols = slice(hh * LANES, (hh + 1) * LANES)
        kd_ref[:, hh, :] = kd[:, cols]
        vd_ref[:, hh, :] = vd[:, cols]
    kdb_ref[...] = kd.astype(BF16)
    sub = vdt_ref.shape[2]
    for u in range(vdt_ref.shape[0]):
        vdt_ref[u] = vd[u * sub:(u + 1) * sub].T.astype(BF16)

    cq = _rms(mm(_C_CQ, c_ckv), gq_ref[...]).astype(BF16)
    q1 = _dot(cq, wuq1_ref[...])
    q2 = _dot(cq, wuq2_ref[...])
    cosq = cosq_ref[...]
    sinq = sinq_ref[...]
    for hh in range(MLA_HEADS):
        sl = slice(hh * LANES, (hh + 1) * LANES)
        qm_ref[:, sl] = (q1[:, sl] * cosq + q2[:, sl] * sinq).astype(BF16)

    ckv = _rms(mm(c_ckv, c_kpe), gkv_ref[...])
    ckv_ref[...] = ckv
    cb = ckv.astype(BF16)
    r = mm(c_kpe, c_kpe + LANES) * tabk_ref[...]
    kr = r + pltpu.roll(r, 3 * LANES // 4, 1)
    kpe_ref[...] = kr[:, :MLA_ROPE]
    km_ref[...] = (_dot(cb, wukp_ref[...]) + _dot(kr.astype(BF16), ekpe_ref[...])).astype(BF16)
    for u in range(vmt_ref.shape[0]):
        vmt_ref[u] = _dot_nt(wuvt_ref[...], cb[u * sub:(u + 1) * sub]).astype(BF16)

    gt_ref[...] = jax.nn.sigmoid(_dot(hb, wg_ref[...])).astype(BF16)


def _inproj(x2d, sc, sh, w, tabs, tm, rows_per_mod_block):
    t, d = x2d.shape
    q_rank = w["gq"].shape[1]
    kv_rank = w["gkv"].shape[1]
    rmod = sc.shape[1]
    n_tab_blocks = tabs[0].shape[0] // tm
    n_tiles = t // tm
    row = lambda width: pl.BlockSpec((tm, width), lambda i: (i, 0))
    heads = pl.BlockSpec((tm, DA_HEADS, LANES), lambda i: (i, 0, 0))
    sub = min(tm, TILE)
    n_sub = tm // sub
    transposed = lambda width: pl.BlockSpec((n_sub, width, sub), lambda i: (i, 0, 0))
    mod_spec = pl.BlockSpec((1, rmod, d), lambda i: (i // rows_per_mod_block, 0, 0))
    tab_spec = pl.BlockSpec((tm, LANES), lambda i: (i % n_tab_blocks, 0))
    weights = [w["g0"], w["w_main"], w["w_g"], w["gq"], w["gkv"], w["w_uq1"], w["w_uq2"], w["w_ukp"], w["w_uvt"],
               w["e_kpe"]]
    in_specs = ([row(d), mod_spec, mod_spec] + [_const_spec(a.shape) for a in weights] + [tab_spec] * 3)
    sds = jax.ShapeDtypeStruct
    outs = [
        (row(DA_W), sds((t, DA_W), BF16)),
        (heads, sds((t, DA_HEADS, LANES), F32)),
        (heads, sds((t, DA_HEADS, LANES), F32)),
        (row(DA_W), sds((t, DA_W), BF16)),
        (transposed(DA_W), sds((n_tiles * n_sub, DA_W, sub), BF16)),
        (row(kv_rank), sds((t, kv_rank), F32)),
        (row(MLA_ROPE), sds((t, MLA_ROPE), F32)),
        (row(MLA_W), sds((t, MLA_W), BF16)),
        (row(MLA_W), sds((t, MLA_W), BF16)),
        (transposed(MLA_VW), sds((n_tiles * n_sub, MLA_VW, sub), BF16)),
        (row(2 * d), sds((t, 2 * d), BF16)),
    ]
    return pl.pallas_call(
        functools.partial(_inproj_kernel, q_rank, kv_rank),
        grid=(n_tiles,),
        in_specs=in_specs,
        out_specs=[o[0] for o in outs],
        out_shape=[o[1] for o in outs],
        compiler_params=pltpu.CompilerParams(dimension_semantics=("arbitrary",), vmem_limit_bytes=VMEM_LIMIT),
        name="inproj",
    )(x2d, sc, sh, *weights, *tabs)


def _t5_bucket(dist):
    n = jnp.maximum(dist, 0)
    max_exact = REL_BUCKETS // 2
    nf = jnp.maximum(n, max_exact).astype(F32)
    large = max_exact + (jnp.log(nf / max_exact) / math.log(REL_MAX_DIST / max_exact)
                         * (REL_BUCKETS - max_exact)).astype(jnp.int32)
    large = jnp.minimum(large, REL_BUCKETS - 1)
    return jnp.where(n < max_exact, n, large)


def _shifted_bias(relb_ref, bucket, head):
    last = relb_ref[REL_BUCKETS - 1, head]
    out = jnp.zeros(bucket.shape, F32)
    for b in range(REL_BUCKETS - 1):
        out = jnp.where(bucket == b, (relb_ref[b, head] - last) * LOG2E, out)
    return out


def _lambda_full(lam_ref, lam_init):
    lf = lam_ref[...]
    a = jnp.sum(lf[0:1] * lf[1:2], axis=1, keepdims=True)
    b = jnp.sum(lf[2:3] * lf[3:4], axis=1, keepdims=True)
    return jnp.exp(a) - jnp.exp(b) + lam_init


def _attn_kernel(lam_init, past, page, n_groups, n_chunks, n_steps, chunks_per_step,
                 pt_ref,
                 relb_ref, lam_ref, subg_ref, qd_ref, qm_ref, kd_ref, vdt_ref, km_ref, vmt_ref,
                 q8_ref, qlat_ref, qpe_ref, knew_ref, vnew_ref, cnew_ref, pnew_ref,
                 ck_hbm, cv_hbm, cc_hbm, cp_hbm,
                 oa_ref, ob_ref, oas_ref, olat_ref,
                 bias_ref, qh_ref, m_ref, l_ref, acc_da_ref, acc_ml_ref,
                 kbuf, vbuf, cbuf, pbuf, sem, mda_ref, lda_ref, ada_ref, mml_ref, lml_ref, aml_ref, near_ref):
    t = TILE
    bi = pl.program_id(0)
    qi = pl.program_id(1)
    nq = pl.num_programs(1)
    n_da = 2 * DA_HEADS
    npg = CHUNK_PAGES
    lam = _lambda_full(lam_ref, lam_init)
    rows_pair = page * DA_HEADS // 2
    row1 = lax.broadcasted_iota(jnp.int32, (ROWS, LANES), 0)
    lane1 = lax.broadcasted_iota(jnp.int32, (ROWS, LANES), 1)
    rowc = lax.broadcasted_iota(jnp.int32, (ROWS, rows_pair), 0)
    colc = lax.broadcasted_iota(jnp.int32, (ROWS, rows_pair), 1)
    own = (colc & 1) == lax.shift_right_logical(rowc & (SUBLANES - 1), 1)

    def chunk_copies(c, slot):
        seq = lax.div(c, n_groups)
        first = lax.rem(c, n_groups) * npg
        copies = []
        for p in range(npg):
            pg = pt_ref[seq, first + p]
            copies += [pltpu.make_async_copy(ck_hbm.at[pg], kbuf.at[slot, p], sem.at[slot]),
                       pltpu.make_async_copy(cv_hbm.at[pg], vbuf.at[slot, p], sem.at[slot]),
                       pltpu.make_async_copy(cc_hbm.at[pg], cbuf.at[slot, p], sem.at[slot]),
                       pltpu.make_async_copy(cp_hbm.at[pg], pbuf.at[slot, p], sem.at[slot])]
        return copies

    def start_chunk(c, slot):
        copies = chunk_copies(c, slot)
        per_page = len(copies) // npg
        for i, cp in enumerate(copies):
            cp.start(priority=(i // per_page) % 2)

    def head_bias(bucket):
        out = jnp.zeros(bucket.shape, F32)
        r = lax.broadcasted_iota(jnp.int32, bucket.shape, 0) & (SUBLANES - 1)
        for hh in range(DA_HEADS):
            out = jnp.where(r == hh, _shifted_bias(relb_ref, bucket, hh), out)
        return out

    def softmax_update(s_list, mx_ref, sum_ref, a_ref, values):
        m_old = mx_ref[...]
        m_new = m_old
        for s in s_list:
            m_new = jnp.maximum(m_new, jnp.max(s, axis=1, keepdims=True))
        alpha = jnp.exp2(m_old - m_new)
        l_new = alpha * sum_ref[...]
        acc = alpha * a_ref[...]
        for s, v in zip(s_list, values):
            p = jnp.exp2(s - m_new)
            l_new = l_new + jnp.sum(p, axis=1, keepdims=True)
            acc = acc + _dot(p.astype(BF16), v)
        mx_ref[...] = m_new
        sum_ref[...] = l_new
        a_ref[...] = acc

    def chunk_begin(c):
        slot = lax.rem(c, CHUNK_SLOTS)
        for cp in chunk_copies(c, slot):
            cp.wait()
        seq = lax.div(c, n_groups)
        grp = lax.rem(c, n_groups)
        q8 = q8_ref[seq].astype(F32)
        q16 = jnp.concatenate([q8, q8], axis=0)
        qhalf = jnp.where(lax.shift_right_logical(lane1, 6) == lax.shift_right_logical(row1, 3), q16, 0.0)
        qrows = jnp.concatenate([jnp.where((row1 & 1) == 0, qhalf, 0.0), jnp.where((row1 & 1) == 1, qhalf, 0.0)],
                                axis=1).astype(BF16)
        qlat = qlat_ref[seq]
        qpe = qpe_ref[seq]

        @pl.when(grp == 0)
        def _():
            kn = knew_ref[seq].astype(BF16).astype(F32)
            kn = jnp.concatenate([kn, kn], axis=0)
            mda_ref[...] = (jnp.sum(qhalf.astype(BF16).astype(F32) * kn, axis=1, keepdims=True)
                            + head_bias(jnp.zeros((ROWS, 1), jnp.int32)))
            lda_ref[...] = jnp.ones((ROWS, 1), F32)
            vn = vnew_ref[seq].astype(BF16).astype(F32)
            vn = jnp.concatenate([vn, vn], axis=0)
            ada_ref[...] = jnp.concatenate([vn, vn], axis=1)
            cn = cnew_ref[pl.ds(seq, 1), :].astype(BF16).astype(F32)
            pn = pnew_ref[pl.ds(seq, 1), :].astype(BF16).astype(F32)
            mml_ref[...] = (jnp.sum(qlat.astype(F32) * cn, axis=1, keepdims=True)
                            + jnp.sum(qpe.astype(F32) * pn, axis=1, keepdims=True))
            lml_ref[...] = jnp.ones((ROWS, 1), F32)
            aml_ref[...] = jnp.broadcast_to(cn, aml_ref.shape)

        return dict(c=c, slot=slot, seq=seq, grp=grp, qrows=qrows, qlat=qlat, qpe=qpe)

    def paired(buf, slot, i):
        even = buf[slot, i, pl.ds(0, rows_pair, stride=2), :]
        odd = buf[slot, i, pl.ds(1, rows_pair, stride=2), :]
        return jnp.concatenate([even, odd], axis=1).astype(BF16)

    def chunk_page_scores(ch, i):
        slot = ch["slot"]
        cb = cbuf[slot, i].astype(BF16)
        s_da = jnp.where(own, _dot_nt(ch["qrows"], paired(kbuf, slot, i)), NEG)
        if i == npg - 1:
            near = near_ref[...]
            s_da = s_da + jnp.where(ch["grp"] == n_groups - 1, near, jnp.zeros_like(near))
        ch.setdefault("cb", {})[i] = cb
        ch.setdefault("s_da", {})[i] = s_da
        ch.setdefault("s_ml", {})[i] = _dot_nt(ch["qlat"], cb) + _dot(ch["qpe"], pbuf[slot, i].astype(BF16))

    def chunk_sub_update(ch, k):
        slot = ch["slot"]
        pages = range(k * SUB_PAGES, (k + 1) * SUB_PAGES)
        softmax_update([ch["s_da"][i] for i in pages], mda_ref, lda_ref, ada_ref,
                       [paired(vbuf, slot, i) for i in pages])
        softmax_update([ch["s_ml"][i] for i in pages], mml_ref, lml_ref, aml_ref, [ch["cb"][i] for i in pages])
        if k == npg // SUB_PAGES - 1:
            start_chunk(jnp.minimum(ch["c"] + CHUNK_SLOTS, n_chunks - 1), slot)

    def chunk_end(ch):
        @pl.when(ch["grp"] == n_groups - 1)
        def _():
            accn = ada_ref[...] * (1.0 / lda_ref[...])
            accn = jnp.where((row1 & 1) == 0, accn[:, :LANES], accn[:, LANES:])
            o = accn[:DA_HEADS] - lam * accn[SUBLANES:SUBLANES + DA_HEADS]
            oas_ref[ch["seq"]] = _rms(o, subg_ref[...]) * (1.0 - lam_init)
            olat_ref[ch["seq"]] = (aml_ref[...] * (1.0 / lml_ref[...]))[:MLA_HEADS]

    @pl.when((bi == 0) & (qi == 0))
    def _():
        for c in range(min(CHUNK_SLOTS, n_chunks)):
            start_chunk(c, c)
        last_page_pos = (n_groups * npg - 1) * page + lax.shift_right_logical(colc, 1)
        near_ref[...] = head_bias(_t5_bucket(past - last_page_pos))

    @pl.when((bi == 0) & (qi == 0))
    def _():
        key = lax.broadcasted_iota(jnp.int32, (t, t), 0)
        qry = lax.broadcasted_iota(jnp.int32, (t, t), 1)
        d0 = qry - key
        b0 = _t5_bucket(d0)
        b1 = _t5_bucket(d0 + t)
        for hh in range(DA_HEADS):
            bias_ref[hh] = jnp.where(d0 >= 0, _shifted_bias(relb_ref, b0, hh), NEG)
            bias_ref[DA_HEADS + hh] = _shifted_bias(relb_ref, b1, hh)
        bias_ref[2 * DA_HEADS] = jnp.where(d0 >= 0, 0.0, NEG)

    lane = lax.broadcasted_iota(jnp.int32, (t, LANES), 1)
    for hh in range(DA_HEADS):
        q = qd_ref[:, hh * LANES:(hh + 1) * LANES].astype(F32)
        qh_ref[2 * hh] = jnp.where(lane < DA_DH, q, 0.0).astype(BF16)
        qh_ref[2 * hh + 1] = jnp.where(lane >= DA_DH, q, 0.0).astype(BF16)

    m_ref[...] = jnp.full(m_ref.shape, NEG, F32)
    l_ref[...] = jnp.zeros(l_ref.shape, F32)
    acc_da_ref[...] = jnp.zeros(acc_da_ref.shape, F32)
    acc_ml_ref[...] = jnp.zeros(acc_ml_ref.shape, F32)

    first_step = bi * ((nq * (nq + 1)) // 2) + (qi * (qi + 1)) // 2

    def step(j, kind):
        n = first_step + j
        if chunks_per_step == 1:
            c = lax.div(n * n_chunks, n_steps)
            has_chunks = lax.div((n + 1) * n_chunks, n_steps) > c
            chunk_ids = [c]
        else:
            has_chunks = (n + 1) * chunks_per_step <= n_chunks
            chunk_ids = [n * chunks_per_step + u for u in range(chunks_per_step)]

        @pl.when(has_chunks)
        def _():
            flash_step(j, kind, chunk_ids)

        @pl.when(jnp.logical_not(has_chunks))
        def _():
            flash_step(j, kind, [])

    def flash_step(j, kind, chunk_ids):
        chunks = [chunk_begin(c) for c in chunk_ids]
        page_jobs = [(ch, p) for ch in chunks for p in range(npg)]
        update_jobs = [(ch, k) for ch in chunks for k in range(npg // SUB_PAGES)]

        def deal(jobs, i):
            per_map = -(-len(jobs) // N_MAPS)
            return jobs[i * per_map:(i + 1) * per_map]

        scores = []
        for i in range(N_MAPS):
            if i < n_da:
                hh = i // 2
                k = kd_ref[0, j, :, hh * LANES:(hh + 1) * LANES]
                q = qh_ref[i]
                bias = {"far": None, "near": DA_HEADS + hh, "diag": hh}[kind]
            else:
                hh = i - n_da
                k = km_ref[0, j, :, hh * LANES:(hh + 1) * LANES]
                q = qm_ref[:, hh * LANES:(hh + 1) * LANES]
                bias = {"far": None, "near": None, "diag": 2 * DA_HEADS}[kind]
            s = _dot_nt(k, q)
            if bias is not None:
                s = s + bias_ref[bias]
            scores.append(s)
            for ch, p in deal(page_jobs, i):
                chunk_page_scores(ch, p)
        probs = []
        for i in range(N_MAPS):
            s = scores[i]
            m_old = m_ref[i]
            m_new = jnp.maximum(m_old, jnp.max(s, axis=0, keepdims=True))
            alpha = jnp.exp2(m_old - m_new)
            p = jnp.exp2(s - m_new)
            l_ref[i] = alpha * l_ref[i] + jnp.sum(p, axis=0, keepdims=True)
            m_ref[i] = m_new
            probs.append((alpha, p.astype(BF16)))
        maps_per_update = max(1, N_MAPS // max(1, len(update_jobs)))
        for i in range(N_MAPS):
            if i % maps_per_update == 0 and i // maps_per_update < len(update_jobs):
                chunk_sub_update(*update_jobs[i // maps_per_update])
            alpha, p = probs[i]
            if i < n_da:
                hh = i // 2
                vt = vdt_ref[j, hh * LANES:(hh + 1) * LANES, :]
                acc_ref, ai = acc_da_ref, i
            else:
                hh = i - n_da
                vt = vmt_ref[j, hh * MLA_VD:(hh + 1) * MLA_VD, :]
                acc_ref, ai = acc_ml_ref, hh
            acc_ref[ai] = alpha * acc_ref[ai] + _dot(vt, p)
        for job in update_jobs[-(-N_MAPS // maps_per_update):]:
            chunk_sub_update(*job)
        for ch in chunks:
            chunk_end(ch)

    def far(j, carry):
        step(j, "far")
        return carry

    lax.fori_loop(0, qi - 1, far, 0)

    @pl.when(qi >= 1)
    def _():
        step(qi - 1, "near")

    step(qi, "diag")

    @pl.when((bi == pl.num_programs(0) - 1) & (qi == nq - 1))
    def _():
        for slot in range(min(CHUNK_SLOTS, n_chunks)):
            for cp in chunk_copies(n_chunks - 1, slot):
                cp.wait()

    for hh in range(DA_HEADS):
        ot = (acc_da_ref[2 * hh] * (1.0 / l_ref[2 * hh])
              - lam * (acc_da_ref[2 * hh + 1] * (1.0 / l_ref[2 * hh + 1])))
        oa_ref[:, hh * LANES:(hh + 1) * LANES] = (_rms(ot.T, subg_ref[...]) * (1.0 - lam_init)).astype(BF16)
    for pair in range(MLA_HEADS // 2):
        ha, hb = 2 * pair, 2 * pair + 1
        ot = jnp.concatenate([acc_ml_ref[ha] * (1.0 / l_ref[n_da + ha]),
                              acc_ml_ref[hb] * (1.0 / l_ref[n_da + hb])], axis=0)
        ob_ref[:, pair * LANES:(pair + 1) * LANES] = ot.T.astype(BF16)


def _attention(lam_init, rel_bias, da_lambda, subg, qd, qm, kdb, vdt, km, vmt, batch, seq,
               page_table, q8, qlat, qpe, knew, vnew, cnew, pnew, cache_k, cache_v, cache_c, cache_pt):
    t = TILE
    nq = seq // t
    n_seq, n_pages = page_table.shape
    page = cache_c.shape[1]
    kv_rank = cache_c.shape[2]
    npg = CHUNK_PAGES
    assert n_pages % npg == 0 and page >= BIAS_CUTOFF
    n_groups = n_pages // npg
    n_chunks = n_seq * n_groups
    n_steps = batch * (nq * (nq + 1)) // 2
    chunks_per_step = -(-n_chunks // n_steps)
    assert n_groups % chunks_per_step == 0 and n_chunks * (n_steps + 1) < 2 ** 31
    past = n_pages * page

    tiles = lambda a: a.reshape(batch, nq, t, a.shape[-1])
    once = pl.Buffered(1)
    kv_spec = lambda width: pl.BlockSpec((1, nq, t, width), lambda b, i, pt: (b, 0, 0, 0), pipeline_mode=once)
    vt_spec = lambda width: pl.BlockSpec((nq, width, t), lambda b, i, pt: (b, 0, 0), pipeline_mode=once)
    q_spec = lambda width: pl.BlockSpec((t, width), lambda b, i, pt: (b * nq + i, 0))
    whole = lambda a: pl.BlockSpec(a.shape, lambda b, i, pt: (0,) * a.ndim, pipeline_mode=once)
    hbm = pl.BlockSpec(memory_space=pl.ANY)
    sds = jax.ShapeDtypeStruct
    out_shape = [sds((batch * seq, DA_W), BF16), sds((batch * seq, MLA_VW), BF16),
                 sds((n_seq, DA_HEADS, LANES), F32), sds((n_seq, MLA_HEADS, kv_rank), F32)]
    grid_spec = pltpu.PrefetchScalarGridSpec(
        num_scalar_prefetch=1,
        grid=(batch, nq),
        in_specs=[pl.BlockSpec(memory_space=pltpu.SMEM), whole(da_lambda), whole(subg),
                  q_spec(DA_W), q_spec(MLA_W), kv_spec(DA_W), vt_spec(DA_W), kv_spec(MLA_W), vt_spec(MLA_VW),
                  whole(q8), whole(qlat), whole(qpe), whole(knew), whole(vnew), whole(cnew), whole(pnew),
                  hbm, hbm, hbm, hbm],
        out_specs=[q_spec(DA_W), q_spec(MLA_VW),
                   pl.BlockSpec(out_shape[2].shape, lambda b, i, pt: (0, 0, 0)),
                   pl.BlockSpec(out_shape[3].shape, lambda b, i, pt: (0, 0, 0))],
        scratch_shapes=[pltpu.VMEM((2 * DA_HEADS + 1, t, t), F32), pltpu.VMEM((2 * DA_HEADS, t, LANES), BF16),
                        pltpu.VMEM((N_MAPS, 1, t), F32), pltpu.VMEM((N_MAPS, 1, t), F32),
                        pltpu.VMEM((2 * DA_HEADS, LANES, t), F32), pltpu.VMEM((MLA_HEADS, MLA_VD, t), F32),
                        pltpu.VMEM((CHUNK_SLOTS, npg, page * DA_HEADS, LANES), F32),
                        pltpu.VMEM((CHUNK_SLOTS, npg, page * DA_HEADS, LANES), F32),
                        pltpu.VMEM((CHUNK_SLOTS, npg, page, kv_rank), F32),
                        pltpu.VMEM((CHUNK_SLOTS, npg, MLA_ROPE, page), F32),
                        pltpu.SemaphoreType.DMA((CHUNK_SLOTS,)),
                        pltpu.VMEM((ROWS, 1), F32), pltpu.VMEM((ROWS, 1), F32), pltpu.VMEM((ROWS, 2 * LANES), F32),
                        pltpu.VMEM((ROWS, 1), F32), pltpu.VMEM((ROWS, 1), F32), pltpu.VMEM((ROWS, kv_rank), F32),
                        pltpu.VMEM((ROWS, page * DA_HEADS // 2), F32)],
    )
    return pl.pallas_call(
        functools.partial(_attn_kernel, lam_init, past, page, n_groups, n_chunks, n_steps, chunks_per_step),
        grid_spec=grid_spec,
        out_shape=out_shape,
        compiler_params=pltpu.CompilerParams(dimension_semantics=("arbitrary", "arbitrary"),
                                             vmem_limit_bytes=VMEM_LIMIT),
        name="attention",
    )(page_table, rel_bias, da_lambda, subg, qd, qm, tiles(kdb), vdt, tiles(km), vmt,
      q8, qlat, qpe, knew, vnew, cnew, pnew, cache_k, cache_v, cache_c, cache_pt)


def _headproj_kernel(n_heads, in_w, out_w, x_ref, w_ref, o_ref):
    for hh in range(n_heads):
        x = x_ref[:, hh * in_w:(hh + 1) * in_w].astype(BF16)
        o_ref[:, hh * out_w:(hh + 1) * out_w] = _dot(x, w_ref[hh]).astype(o_ref.dtype)


def _headproj(x, w, out_dtype, name):
    n_heads, in_w, out_w = w.shape
    r = x.shape[0]
    return pl.pallas_call(
        functools.partial(_headproj_kernel, n_heads, in_w, out_w),
        out_shape=jax.ShapeDtypeStruct((r, n_heads * out_w), out_dtype),
        compiler_params=pltpu.CompilerParams(vmem_limit_bytes=VMEM_LIMIT),
        name=name,
    )(x, w)


def _post_kernel(d_ff, x_ref, oa_ref, ob_ref, gt_ref, gt1_ref, sh2_ref, sc2_ref, gt2_ref, ng_ref,
                 wa_ref, wb_ref, wo_ref, wgu_ref, wdn_ref, y_ref):
    d = x_ref.shape[1]
    x = x_ref[...]
    gates = gt_ref[...]
    merged = (gates[:, :d].astype(F32) * _dot(oa_ref[...], wa_ref[...])
              + gates[:, d:].astype(F32) * _dot(ob_ref[...], wb_ref[...]))
    x1 = x + gt1_ref[0] * _rms(_dot(merged.astype(BF16), wo_ref[...]), ng_ref[1:2])
    h2 = (_rms(x1, ng_ref[2:3]) * (1.0 + sc2_ref[0]) + sh2_ref[0]).astype(BF16)
    gg = _dot(h2, wgu_ref[:, :d_ff])
    uu = _dot(h2, wgu_ref[:, d_ff:])
    act = (gg * jax.nn.sigmoid(gg) * uu).astype(BF16)
    y_ref[...] = x1 + gt2_ref[0] * _rms(_dot(act, wdn_ref[...]), ng_ref[3:4])


def _post(x2d, oa, ob, gates, mods, w, tm, rows_per_mod_block):
    t, d = x2d.shape
    d_ff = w["w_dn"].shape[0]
    rmod = mods[0].shape[1]
    row = lambda width: pl.BlockSpec((tm, width), lambda i: (i, 0))
    mod_spec = pl.BlockSpec((1, rmod, d), lambda i: (i // rows_per_mod_block, 0, 0))
    weights = [w["ng"], w["w_a"], w["w_b"], w["w_o"], w["w_gu"], w["w_dn"]]
    return pl.pallas_call(
        functools.partial(_post_kernel, d_ff),
        grid=(t // tm,),
        in_specs=[row(d), row(oa.shape[1]), row(ob.shape[1]), row(2 * d)] + [mod_spec] * 4
                 + [_const_spec(a.shape) for a in weights],
        out_specs=row(d),
        out_shape=jax.ShapeDtypeStruct((t, d), F32),
        compiler_params=pltpu.CompilerParams(dimension_semantics=("arbitrary",), vmem_limit_bytes=VMEM_LIMIT),
        name="post",
    )(x2d, oa, ob, gates, *mods, *weights)


def _layer_weights(norm_g, w_in, mla_q_norm_g, mla_kv_norm_g, mla_w_uq, mla_w_uk, mla_w_uv, w_branch_a,
                   w_branch_b, w_o, ffn_w_gu, ffn_w_down):
    d = w_in.shape[0]
    q_rank = mla_q_norm_g.shape[0]
    kv_rank = mla_kv_norm_g.shape[0]
    half = MLA_ROPE // 2
    c0 = 3 * DA_W + q_rank + kv_rank
    w_kpe = w_in[:, c0:c0 + MLA_ROPE]
    partner = jnp.concatenate([w_kpe[:, half:], w_kpe[:, :half]], axis=1)
    w_main = jnp.concatenate([w_in[:, :c0], w_kpe, partner, jnp.zeros((d, MLA_ROPE), F32), partner], axis=1)
    w_g = w_in[:, c0 + MLA_ROPE:]

    pad = LANES - MLA_NOPE - MLA_ROPE
    zq = lambda n: jnp.zeros((q_rank, MLA_HEADS, n), F32)
    nope, r1, r2 = mla_w_uq[..., :MLA_NOPE], mla_w_uq[..., MLA_NOPE:MLA_NOPE + half], mla_w_uq[..., MLA_NOPE + half:]
    w_uq1 = jnp.concatenate([nope, r1, r2, zq(pad)], axis=-1).reshape(q_rank, MLA_W)
    w_uq2 = jnp.concatenate([zq(MLA_NOPE), r2, r1, zq(pad)], axis=-1).reshape(q_rank, MLA_W)
    zk = lambda n: jnp.zeros((kv_rank, MLA_HEADS, n), F32)
    w_ukp = jnp.concatenate([mla_w_uk, zk(LANES - MLA_NOPE)], axis=-1).reshape(kv_rank, MLA_W)
    e = np.zeros((LANES, MLA_HEADS, LANES), np.float32)
    for l in range(MLA_ROPE):
        e[l, :, MLA_NOPE + l] = 1.0
    e_kpe = jnp.asarray(e.reshape(LANES, MLA_W))
    w_uvt = mla_w_uv.reshape(kv_rank, MLA_VW).T
    w_lat = jnp.concatenate([jnp.transpose(mla_w_uk, (1, 2, 0)),
                             jnp.zeros((MLA_HEADS, LANES - MLA_NOPE, kv_rank), F32)], axis=1)
    w_val = jnp.transpose(mla_w_uv, (1, 0, 2))
    bf = lambda a: a.astype(BF16)
    return dict(
        g0=norm_g[0:1], ng=norm_g, gq=mla_q_norm_g[None], gkv=mla_kv_norm_g[None],
        w_main=bf(w_main), w_g=bf(w_g), w_uq1=bf(w_uq1), w_uq2=bf(w_uq2), w_ukp=bf(w_ukp), w_uvt=bf(w_uvt),
        e_kpe=bf(e_kpe), w_lat=bf(w_lat), w_val=bf(w_val),
        w_a=bf(w_branch_a), w_b=bf(w_branch_b), w_o=bf(w_o), w_gu=bf(ffn_w_gu), w_dn=bf(ffn_w_down))


def _rope_tables(pos):
    freqs = ROPE_THETA ** (-jnp.arange(0, MLA_ROPE, 2, dtype=F32) / MLA_ROPE)
    ang = pos.astype(F32)[:, None] * freqs[None, :]
    cos, sin = jnp.cos(ang), jnp.sin(ang)
    n = pos.shape[0]
    one = jnp.ones((n, MLA_NOPE), F32)
    zero = lambda w: jnp.zeros((n, w), F32)
    pad = LANES - MLA_NOPE - MLA_ROPE
    cosq = jnp.concatenate([one, cos, cos, zero(pad)], axis=1) * MLA_SCALE
    sinq = jnp.concatenate([zero(MLA_NOPE), -sin, sin, zero(pad)], axis=1) * MLA_SCALE
    part = jnp.concatenate([-sin, sin], axis=1)
    tabk = jnp.concatenate([cos, cos, part, zero(MLA_ROPE), part], axis=1)
    return cosq, sinq, tabk


def _layer(l, x_prompt, x_sample, cache_k, cache_v, cache_c, cache_p, page_table, c_prompt, c_sample, rel_bias,
           ada_w, ada_b, da_lambda, da_subln_g, w):
    batch, seq, d = x_prompt.shape
    n_seq, dec_seq, _ = x_sample.shape
    assert dec_seq == 1
    lam_init = 0.8 - 0.6 * math.exp(-0.3 * l)
    n_pool, page = cache_k.shape[1:3]
    past = page_table.shape[1] * page
    subg = da_subln_g[None]

    mod = _adaln(jnp.concatenate([c_prompt, c_sample], axis=0), ada_w, ada_b)
    mods_p = [mod[:batch, i * d:(i + 1) * d].reshape(batch, 1, d) for i in range(6)]
    mods_s = [mod[batch:, i * d:(i + 1) * d].reshape(1, n_seq, d) for i in range(6)]

    xp = x_prompt.reshape(batch * seq, d)
    tm = ROW_TILE
    (qd, kd, vd, kdb, vdt, ckv, kpe, qm, km, vmt, gates) = _inproj(
        xp, mods_p[1], mods_p[0], w, _rope_tables(jnp.arange(seq, dtype=jnp.int32)), tm, seq // tm)
    xs = x_sample.reshape(n_seq, d)
    pos_s = jnp.full((n_seq,), past, jnp.int32)
    (qd_s, kd_s, vd_s, _, _, ckv_s, kpe_s, qm_s, _, _, gates_s) = _inproj(
        xs, mods_s[1], mods_s[0], w, _rope_tables(pos_s), n_seq, 1)
    kv_rank = ckv_s.shape[1]
    qlat = _headproj(qm_s, w["w_lat"], BF16, "latent_query").reshape(n_seq, MLA_HEADS, kv_rank)
    qpe = qm_s.reshape(n_seq, MLA_HEADS, LANES)[:, :, MLA_NOPE:MLA_NOPE + MLA_ROPE]
    pad_rows = lambda a, r: jnp.pad(a, ((0, 0), (0, r - a.shape[1]), (0, 0)))

    oa, ob, oa_s, olat = _attention(
        lam_init, rel_bias, da_lambda, subg, qd, qm, kdb, vdt, km, vmt, batch, seq, page_table,
        pad_rows(qd_s.reshape(n_seq, DA_HEADS, LANES), SUBLANES), pad_rows(qlat, ROWS), pad_rows(qpe, ROWS),
        pad_rows(kd_s, SUBLANES), pad_rows(vd_s, SUBLANES), ckv_s, kpe_s,
        cache_k.reshape(n_pool, page * DA_HEADS, LANES), cache_v.reshape(n_pool, page * DA_HEADS, LANES),
        cache_c.reshape(n_pool, page, kv_rank), jnp.swapaxes(cache_p.reshape(n_pool, page, MLA_ROPE), 1, 2))

    y_p = _post(xp, oa, ob, gates, [mods_p[2], mods_p[3], mods_p[4], mods_p[5]], w, tm, seq // tm)
    ob_s = _headproj(olat.reshape(n_seq, MLA_HEADS * kv_rank), w["w_val"], BF16, "value_up")
    y_s = _post(xs, oa_s.reshape(n_seq, DA_W).astype(BF16), ob_s, gates_s,
                [mods_s[2], mods_s[3], mods_s[4], mods_s[5]], w, n_seq, 1)
    state_p = (kd.reshape(batch, seq, DA_HEADS, 2 * DA_DH), vd.reshape(batch, seq, DA_HEADS, DA_VD),
               ckv.reshape(batch, seq, -1), kpe.reshape(batch, seq, MLA_ROPE))
    state_s = (kd_s.reshape(n_seq, 1, DA_HEADS, 2 * DA_DH), vd_s.reshape(n_seq, 1, DA_HEADS, DA_VD),
               ckv_s.reshape(n_seq, 1, -1), kpe_s.reshape(n_seq, 1, MLA_ROPE))
    return y_p.reshape(batch, seq, d), y_s.reshape(n_seq, 1, d), state_p, state_s


def kernel(x_prompt, x_sample, cache_da_k, cache_da_v, cache_mla_ckv, cache_mla_kpe, page_table, c_prompt, c_sample, rel_bias, ada_w, ada_b, norm_g, w_in, da_lambda, da_subln_g, mla_q_norm_g, mla_kv_norm_g, mla_w_uq, mla_w_uk, mla_w_uv, w_branch_a, w_branch_b, w_o, ffn_w_gu, ffn_w_down):
    depth = ada_w.shape[0]
    y_p, y_s = x_prompt, x_sample
    st_p, st_s = [], []
    for l in range(depth):
        w = _layer_weights(norm_g[l], w_in[l], mla_q_norm_g[l], mla_kv_norm_g[l], mla_w_uq[l], mla_w_uk[l],
                           mla_w_uv[l], w_branch_a[l], w_branch_b[l], w_o[l], ffn_w_gu[l], ffn_w_down[l])
        y_p, y_s, sp, ss = _layer(l, y_p, y_s, cache_da_k[l:l + 1], cache_da_v[l:l + 1], cache_mla_ckv[l:l + 1],
                                  cache_mla_kpe[l:l + 1], page_table, c_prompt, c_sample, rel_bias, ada_w[l],
                                  ada_b[l], da_lambda[l], da_subln_g[l], w)
        st_p.append(sp)
        st_s.append(ss)
    stack = lambda sts, i: jnp.stack([s[i] for s in sts])
    return (y_p, y_s, stack(st_p, 0), stack(st_p, 1), stack(st_p, 2), stack(st_p, 3),
            stack(st_s, 0), stack(st_s, 1), stack(st_s, 2), stack(st_s, 3))
```

```python
import functools
import math

import numpy as np
import jax
import jax.numpy as jnp
from jax import lax
from jax.experimental import pallas as pl
from jax.experimental.pallas import tpu as pltpu

F32 = jnp.float32
BF16 = jnp.bfloat16

DA_HEADS = 4
DA_DH = 64
DA_VD = 2 * DA_DH
MLA_HEADS = 8
MLA_NOPE = 64
MLA_ROPE = 32
MLA_VD = 64
ROPE_HALF = MLA_ROPE // 2
ROPE_THETA = 10000.0
REL_BUCKETS = 32
REL_MAX_DIST = 128
EPS = 1e-6
LOG2E = math.log2(math.e)
DA_SCALE = DA_DH ** -0.5 * LOG2E
MLA_SCALE = (MLA_NOPE + MLA_ROPE) ** -0.5 * LOG2E

LANES = 128
SUBLANES = 8
DA_W = DA_HEADS * LANES
MLA_W = MLA_HEADS * LANES
MLA_VW = MLA_HEADS * MLA_VD
BIAS_CUTOFF = 113
NEG = -1e30
VMEM_LIMIT = 56 * 1024 * 1024

TILE = 256
ROW_TILE = 512
N_MAPS = 2 * DA_HEADS + MLA_HEADS
CHUNK_PAGES = 16
SUB_PAGES = 4
CHUNK_SLOTS = 2
ROWS = 16

NT_DIMS = (((1,), (1,)), ((), ()))


def _rms(x, g):
    return x * lax.rsqrt(jnp.mean(x * x, axis=-1, keepdims=True) + EPS) * g


def _dot(a, b):
    return jnp.dot(a, b, preferred_element_type=F32)


def _dot_nt(a, b):
    return lax.dot_general(a, b, NT_DIMS, preferred_element_type=F32)


def _const_spec(shape):
    nd = len(shape)
    return pl.BlockSpec(shape, lambda *_: (0,) * nd)


def _adaln_kernel(c_ref, w_ref, b_ref, o_ref):
    c = c_ref[...]
    a = (c * jax.nn.sigmoid(c)).astype(BF16)
    o_ref[...] = _dot(a, w_ref[...].astype(BF16)) + b_ref[...]


def _adaln(c_all, ada_w, ada_b):
    r, d = c_all.shape
    n = ada_w.shape[1]
    tn = 768
    return pl.pallas_call(
        _adaln_kernel,
        grid=(n // tn,),
        in_specs=[pl.BlockSpec((r, d), lambda j: (0, 0)),
                  pl.BlockSpec((d, tn), lambda j: (0, j)),
                  pl.BlockSpec((1, tn), lambda j: (0, j))],
        out_specs=pl.BlockSpec((r, tn), lambda j: (0, j)),
        out_shape=jax.ShapeDtypeStruct((r, n), F32),
        compiler_params=pltpu.CompilerParams(dimension_semantics=("arbitrary",), vmem_limit_bytes=VMEM_LIMIT),
        name="adaln",
    )(c_all, ada_w, ada_b.reshape(1, n))


_C_QD, _C_KD, _C_VD = 0, DA_W, 2 * DA_W
_C_CQ = 3 * DA_W


def _inproj_kernel(q_rank, kv_rank,
                   x_ref, sc_ref, sh_ref, g0_ref, wmain_ref, wg_ref, gq_ref, gkv_ref,
                   wuq_ref, wukp_ref, wuvt_ref, cosq_ref, sinq_ref, tabk_ref,
                   qd_ref, kd_ref, vd_ref, kdb_ref, vdt_ref, ckv_ref, kpe_ref, qm_ref, km_ref, vmt_ref, gt_ref):
    c_ckv = _C_CQ + q_rank
    c_kpe = c_ckv + kv_rank
    n_sub, _, sub = vdt_ref.shape
    parts = [slice(u * sub, (u + 1) * sub) for u in range(n_sub)]

    def mod(ref, rows):
        return ref[0] if ref.shape[1] == 1 else ref[0, rows, :]

    stage1 = []
    for rows in parts:
        h = _rms(x_ref[rows, :], g0_ref[...]) * (1.0 + mod(sc_ref, rows)) + mod(sh_ref, rows)
        hb = h.astype(BF16)
        mm = lambda lo, hi, hb=hb: _dot(hb, wmain_ref[:, lo:hi])
        stage1.append(dict(qd=mm(_C_QD, _C_KD), kd=mm(_C_KD, _C_VD), vd=mm(_C_VD, _C_CQ), cq=mm(_C_CQ, c_ckv),
                           ckv=mm(c_ckv, c_kpe), kx=mm(c_kpe, c_kpe + LANES), gates=_dot(hb, wg_ref[...])))

    stage2 = []
    for u, (rows, s1) in enumerate(zip(parts, stage1)):
        qd_ref[rows, :] = (s1["qd"] * DA_SCALE).astype(BF16)
        for hh in range(DA_HEADS):
            cols = slice(hh * LANES, (hh + 1) * LANES)
            kd_ref[rows, hh, :] = s1["kd"][:, cols]
            vd_ref[rows, hh, :] = s1["vd"][:, cols]
        kdb_ref[rows, :] = s1["kd"].astype(BF16)
        vdt_ref[u] = s1["vd"].T.astype(BF16)
        gt_ref[rows, :] = jax.nn.sigmoid(s1["gates"]).astype(BF16)
        cq = _rms(s1["cq"], gq_ref[...]).astype(BF16)
        ckv = _rms(s1["ckv"], gkv_ref[...])
        ckv_ref[rows, :] = ckv
        r = s1["kx"] * tabk_ref[rows, :]
        kr = r + pltpu.roll(r, LANES // 2, 1)
        kpe_ref[rows, :] = kr[:, :MLA_ROPE]
        stage2.append(dict(cq=cq, cb=ckv.astype(BF16), kr=kr))

    for u, (rows, s2) in enumerate(zip(parts, stage2)):
        q1 = _dot(s2["cq"], wuq_ref[...])
        q2 = pltpu.roll(q1, ROPE_HALF, 1) + pltpu.roll(q1, MLA_W - ROPE_HALF, 1)
        cosq = cosq_ref[rows, :]
        sinq = sinq_ref[rows, :]
        kr = s2["kr"]
        lane = lax.broadcasted_iota(jnp.int32, kr.shape, 1)
        in_rope = (lane >= ROPE_HALF) & (lane < ROPE_HALF + MLA_ROPE)
        kslot = jnp.where(in_rope, pltpu.roll(kr, ROPE_HALF, 1), 0.0)
        knope = _dot(s2["cb"], wukp_ref[...])
        for hh in range(MLA_HEADS):
            sl = slice(hh * LANES, (hh + 1) * LANES)
            qm_ref[rows, sl] = (q1[:, sl] * cosq + q2[:, sl] * sinq).astype(BF16)
            km_ref[rows, sl] = (knope[:, sl] + kslot).astype(BF16)
        vmt_ref[u] = _dot_nt(wuvt_ref[...], s2["cb"]).astype(BF16)


def _inproj(x2d, sc, sh, w, tabs, tm, rows_per_mod_block):
    t, d = x2d.shape
    q_rank = w["gq"].shape[1]
    kv_rank = w["gkv"].shape[1]
    rmod = sc.shape[1]
    n_tab_blocks = tabs[0].shape[0] // tm
    n_tiles = t // tm
    row = lambda width: pl.BlockSpec((tm, width), lambda i: (i, 0))
    heads = pl.BlockSpec((tm, DA_HEADS, LANES), lambda i: (i, 0, 0))
    sub = min(tm, TILE)
    n_sub = tm // sub
    transposed = lambda width: pl.BlockSpec((n_sub, width, sub), lambda i: (i, 0, 0))
    mod_spec = pl.BlockSpec((1, rmod, d), lambda i: (i // rows_per_mod_block, 0, 0))
    tab_spec = pl.BlockSpec((tm, LANES), lambda i: (i % n_tab_blocks, 0))
    weights = [w["g0"], w["w_main"], w["w_g"], w["gq"], w["gkv"], w["w_uq"], w["w_ukp"], w["w_uvt"]]
    in_specs = ([row(d), mod_spec, mod_spec] + [_const_spec(a.shape) for a in weights] + [tab_spec] * 3)
    sds = jax.ShapeDtypeStruct
    outs = [
        (row(DA_W), sds((t, DA_W), BF16)),
        (heads, sds((t, DA_HEADS, LANES), F32)),
        (heads, sds((t, DA_HEADS, LANES), F32)),
        (row(DA_W), sds((t, DA_W), BF16)),
        (transposed(DA_W), sds((n_tiles * n_sub, DA_W, sub), BF16)),
        (row(kv_rank), sds((t, kv_rank), F32)),
        (row(MLA_ROPE), sds((t, MLA_ROPE), F32)),
        (row(MLA_W), sds((t, MLA_W), BF16)),
        (row(MLA_W), sds((t, MLA_W), BF16)),
        (transposed(MLA_VW), sds((n_tiles * n_sub, MLA_VW, sub), BF16)),
        (row(2 * d), sds((t, 2 * d), BF16)),
    ]
    return pl.pallas_call(
        functools.partial(_inproj_kernel, q_rank, kv_rank),
        grid=(n_tiles,),
        in_specs=in_specs,
        out_specs=[o[0] for o in outs],
        out_shape=[o[1] for o in outs],
        compiler_params=pltpu.CompilerParams(dimension_semantics=("arbitrary",), vmem_limit_bytes=VMEM_LIMIT),
        name="inproj",
    )(x2d, sc, sh, *weights, *tabs)


def _t5_bucket(dist):
    n = jnp.maximum(dist, 0)
    max_exact = REL_BUCKETS // 2
    nf = jnp.maximum(n, max_exact).astype(F32)
    large = max_exact + (jnp.log(nf / max_exact) / math.log(REL_MAX_DIST / max_exact)
                         * (REL_BUCKETS - max_exact)).astype(jnp.int32)
    large = jnp.minimum(large, REL_BUCKETS - 1)
    return jnp.where(n < max_exact, n, large)


def _shifted_bias(relb_ref, bucket, head):
    last = relb_ref[REL_BUCKETS - 1, head]
    out = jnp.zeros(bucket.shape, F32)
    for b in range(REL_BUCKETS - 1):
        out = jnp.where(bucket == b, (relb_ref[b, head] - last) * LOG2E, out)
    return out


def _lambda_full(lam_ref, lam_init):
    lf = lam_ref[...]
    a = jnp.sum(lf[0:1] * lf[1:2], axis=1, keepdims=True)
    b = jnp.sum(lf[2:3] * lf[3:4], axis=1, keepdims=True)
    return jnp.exp(a) - jnp.exp(b) + lam_init


def _attn_kernel(lam_init, past, page, n_groups, n_chunks, n_steps, chunks_per_step,
                 pt_ref,
                 relb_ref, lam_ref, subg_ref, qd_ref, qm_ref, kd_ref, vdt_ref, km_ref, vmt_ref,
                 q8_ref, qlat_ref, qpe_ref, knew_ref, vnew_ref, cnew_ref, pnew_ref,
                 ck_hbm, cv_hbm, cc_hbm, cp_hbm,
                 oa_ref, ob_ref, oas_ref, olat_ref,
                 bias_ref, qh_ref, m_ref, l_ref, acc_da_ref, acc_ml_ref,
                 kbuf, vbuf, cbuf, pbuf, sem, mda_ref, lda_ref, ada_ref, mml_ref, lml_ref, aml_ref, near_ref):
    t = TILE
    bi = pl.program_id(0)
    qi = pl.program_id(1)
    nq = pl.num_programs(1)
    n_da = 2 * DA_HEADS
    npg = CHUNK_PAGES
    lam = _lambda_full(lam_ref, lam_init)
    rows_pair = page * DA_HEADS // 2
    row1 = lax.broadcasted_iota(jnp.int32, (ROWS, LANES), 0)
    lane1 = lax.broadcasted_iota(jnp.int32, (ROWS, LANES), 1)
    rowc = lax.broadcasted_iota(jnp.int32, (ROWS, rows_pair), 0)
    colc = lax.broadcasted_iota(jnp.int32, (ROWS, rows_pair), 1)
    own = (colc & 1) == lax.shift_right_logical(rowc & (SUBLANES - 1), 1)

    def chunk_copies(c, slot):
        seq = lax.div(c, n_groups)
        first = lax.rem(c, n_groups) * npg
        copies = []
        for p in range(npg):
            pg = pt_ref[seq, first + p]
            copies += [pltpu.make_async_copy(ck_hbm.at[pg], kbuf.at[slot, p], sem.at[slot]),
                       pltpu.make_async_copy(cv_hbm.at[pg], vbuf.at[slot, p], sem.at[slot]),
                       pltpu.make_async_copy(cc_hbm.at[pg], cbuf.at[slot, p], sem.at[slot]),
                       pltpu.make_async_copy(cp_hbm.at[pg], pbuf.at[slot, p], sem.at[slot])]
        return copies

    def start_chunk(c, slot):
        copies = chunk_copies(c, slot)
        per_page = len(copies) // npg
        for i, cp in enumerate(copies):
            cp.start(priority=(i // per_page) % 2)

    def head_bias(bucket):
        out = jnp.zeros(bucket.shape, F32)
        r = lax.broadcasted_iota(jnp.int32, bucket.shape, 0) & (SUBLANES - 1)
        for hh in range(DA_HEADS):
            out = jnp.where(r == hh, _shifted_bias(relb_ref, bucket, hh), out)
        return out

    def softmax_update(s_list, mx_ref, sum_ref, a_ref, values):
        m_old = mx_ref[...]
        m_new = m_old
        for s in s_list:
            m_new = jnp.maximum(m_new, jnp.max(s, axis=1, keepdims=True))
        alpha = jnp.exp2(m_old - m_new)
        l_new = alpha * sum_ref[...]
        acc = alpha * a_ref[...]
        for s, v in zip(s_list, values):
            p = jnp.exp2(s - m_new)
            l_new = l_new + jnp.sum(p, axis=1, keepdims=True)
            acc = acc + _dot(p.astype(BF16), v)
        mx_ref[...] = m_new
        sum_ref[...] = l_new
        a_ref[...] = acc

    def chunk_begin(c):
        slot = lax.rem(c, CHUNK_SLOTS)
        for cp in chunk_copies(c, slot):
            cp.wait()
        seq = lax.div(c, n_groups)
        grp = lax.rem(c, n_groups)
        q8 = q8_ref[seq].astype(F32)
        q16 = jnp.concatenate([q8, q8], axis=0)
        qhalf = jnp.where(lax.shift_right_logical(lane1, 6) == lax.shift_right_logical(row1, 3), q16, 0.0)
        qrows = jnp.concatenate([jnp.where((row1 & 1) == 0, qhalf, 0.0), jnp.where((row1 & 1) == 1, qhalf, 0.0)],
                                axis=1).astype(BF16)
        qlat = qlat_ref[seq]
        qpe = qpe_ref[seq]

        @pl.when(grp == 0)
        def _():
            kn = knew_ref[seq].astype(BF16).astype(F32)
            kn = jnp.concatenate([kn, kn], axis=0)
            mda_ref[...] = (jnp.sum(qhalf.astype(BF16).astype(F32) * kn, axis=1, keepdims=True)
                            + head_bias(jnp.zeros((ROWS, 1), jnp.int32)))
            lda_ref[...] = jnp.ones((ROWS, 1), F32)
            vn = vnew_ref[seq].astype(BF16).astype(F32)
            vn = jnp.concatenate([vn, vn], axis=0)
            ada_ref[...] = jnp.concatenate([vn, vn], axis=1)
            cn = cnew_ref[pl.ds(seq, 1), :].astype(BF16).astype(F32)
            pn = pnew_ref[pl.ds(seq, 1), :].astype(BF16).astype(F32)
            mml_ref[...] = (jnp.sum(qlat.astype(F32) * cn, axis=1, keepdims=True)
                            + jnp.sum(qpe.astype(F32) * pn, axis=1, keepdims=True))
            lml_ref[...] = jnp.ones((ROWS, 1), F32)
            aml_ref[...] = jnp.broadcast_to(cn, aml_ref.shape)

        return dict(c=c, slot=slot, seq=seq, grp=grp, qrows=qrows, qlat=qlat, qpe=qpe)

    def paired(buf, slot, i):
        even = buf[slot, i, pl.ds(0, rows_pair, stride=2), :]
        odd = buf[slot, i, pl.ds(1, rows_pair, stride=2), :]
        return jnp.concatenate([even, odd], axis=1).astype(BF16)

    def chunk_page_scores(ch, i):
        slot = ch["slot"]
        cb = cbuf[slot, i].astype(BF16)
        s_da = jnp.where(own, _dot_nt(ch["qrows"], paired(kbuf, slot, i)), NEG)
        if i == npg - 1:
            near = near_ref[...]
            s_da = s_da + jnp.where(ch["grp"] == n_groups - 1, near, jnp.zeros_like(near))
        ch.setdefault("cb", {})[i] = cb
        ch.setdefault("s_da", {})[i] = s_da
        ch.setdefault("s_ml", {})[i] = _dot_nt(ch["qlat"], cb) + _dot(ch["qpe"], pbuf[slot, i].astype(BF16))

    def chunk_sub_update(ch, k):
        slot = ch["slot"]
        pages = range(k * SUB_PAGES, (k + 1) * SUB_PAGES)
        softmax_update([ch["s_da"][i] for i in pages], mda_ref, lda_ref, ada_ref,
                       [paired(vbuf, slot, i) for i in pages])
        softmax_update([ch["s_ml"][i] for i in pages], mml_ref, lml_ref, aml_ref, [ch["cb"][i] for i in pages])
        if k == npg // SUB_PAGES - 1:
            start_chunk(jnp.minimum(ch["c"] + CHUNK_SLOTS, n_chunks - 1), slot)

    def chunk_end(ch):
        @pl.when(ch["grp"] == n_groups - 1)
        def _():
            accn = ada_ref[...] * (1.0 / lda_ref[...])
            accn = jnp.where((row1 & 1) == 0, accn[:, :LANES], accn[:, LANES:])
            o = accn[:DA_HEADS] - lam * accn[SUBLANES:SUBLANES + DA_HEADS]
            oas_ref[ch["seq"]] = _rms(o, subg_ref[...]) * (1.0 - lam_init)
            olat_ref[ch["seq"]] = (aml_ref[...] * (1.0 / lml_ref[...]))[:MLA_HEADS]

    @pl.when((bi == 0) & (qi == 0))
    def _():
        for c in range(min(CHUNK_SLOTS, n_chunks)):
            start_chunk(c, c)
        last_page_pos = (n_groups * npg - 1) * page + lax.shift_right_logical(colc, 1)
        near_ref[...] = head_bias(_t5_bucket(past - last_page_pos))

    @pl.when((bi == 0) & (qi == 0))
    def _():
        key = lax.broadcasted_iota(jnp.int32, (t, t), 0)
        qry = lax.broadcasted_iota(jnp.int32, (t, t), 1)
        d0 = qry - key
        b0 = _t5_bucket(d0)
        b1 = _t5_bucket(d0 + t)
        for hh in range(DA_HEADS):
            bias_ref[hh] = jnp.where(d0 >= 0, _shifted_bias(relb_ref, b0, hh), NEG)
            bias_ref[DA_HEADS + hh] = _shifted_bias(relb_ref, b1, hh)
        bias_ref[2 * DA_HEADS] = jnp.where(d0 >= 0, 0.0, NEG)

    lane = lax.broadcasted_iota(jnp.int32, (t, LANES), 1)
    for hh in range(DA_HEADS):
        q = qd_ref[:, hh * LANES:(hh + 1) * LANES].astype(F32)
        qh_ref[2 * hh] = jnp.where(lane < DA_DH, q, 0.0).astype(BF16)
        qh_ref[2 * hh + 1] = jnp.where(lane >= DA_DH, q, 0.0).astype(BF16)

    m_ref[...] = jnp.full(m_ref.shape, NEG, F32)
    l_ref[...] = jnp.zeros(l_ref.shape, F32)
    acc_da_ref[...] = jnp.zeros(acc_da_ref.shape, F32)
    acc_ml_ref[...] = jnp.zeros(acc_ml_ref.shape, F32)

    first_step = bi * ((nq * (nq + 1)) // 2) + (qi * (qi + 1)) // 2

    def step(j, kind):
        n = first_step + j
        if chunks_per_step == 1:
            c = lax.div(n * n_chunks, n_steps)
            has_chunks = lax.div((n + 1) * n_chunks, n_steps) > c
            chunk_ids = [c]
        else:
            has_chunks = (n + 1) * chunks_per_step <= n_chunks
            chunk_ids = [n * chunks_per_step + u for u in range(chunks_per_step)]

        @pl.when(has_chunks)
        def _():
            flash_step(j, kind, chunk_ids)

        @pl.when(jnp.logical_not(has_chunks))
        def _():
            flash_step(j, kind, [])

    def flash_step(j, kind, chunk_ids):
        chunks = [chunk_begin(c) for c in chunk_ids]
        page_jobs = [(ch, p) for ch in chunks for p in range(npg)]
        update_jobs = [(ch, k) for ch in chunks for k in range(npg // SUB_PAGES)]

        def deal(jobs, i):
            per_map = -(-len(jobs) // N_MAPS)
            return jobs[i * per_map:(i + 1) * per_map]

        scores = []
        for i in range(N_MAPS):
            if i < n_da:
                hh = i // 2
                k = kd_ref[0, j, :, hh * LANES:(hh + 1) * LANES]
                q = qh_ref[i]
                bias = {"far": None, "near": DA_HEADS + hh, "diag": hh}[kind]
            else:
                hh = i - n_da
                k = km_ref[0, j, :, hh * LANES:(hh + 1) * LANES]
                q = qm_ref[:, hh * LANES:(hh + 1) * LANES]
                bias = {"far": None, "near": None, "diag": 2 * DA_HEADS}[kind]
            s = _dot_nt(k, q)
            if bias is not None:
                s = s + bias_ref[bias]
            scores.append(s)
            for ch, p in deal(page_jobs, i):
                chunk_page_scores(ch, p)
        probs = []
        for i in range(N_MAPS):
            s = scores[i]
            m_old = m_ref[i]
            m_new = jnp.maximum(m_old, jnp.max(s, axis=0, keepdims=True))
            alpha = jnp.exp2(m_old - m_new)
            p = jnp.exp2(s - m_new)
            l_ref[i] = alpha * l_ref[i] + jnp.sum(p, axis=0, keepdims=True)
            m_ref[i] = m_new
            probs.append((alpha, p.astype(BF16)))
        maps_per_update = max(1, N_MAPS // max(1, len(update_jobs)))
        for i in range(N_MAPS):
            if i % maps_per_update == 0 and i // maps_per_update < len(update_jobs):
                chunk_sub_update(*update_jobs[i // maps_per_update])
            alpha, p = probs[i]
            if i < n_da:
                hh = i // 2
                vt = vdt_ref[j, hh * LANES:(hh + 1) * LANES, :]
                acc_ref, ai = acc_da_ref, i
            else:
                hh = i - n_da
                vt = vmt_ref[j, hh * MLA_VD:(hh + 1) * MLA_VD, :]
                acc_ref, ai = acc_ml_ref, hh
            acc_ref[ai] = alpha * acc_ref[ai] + _dot(vt, p)
        for job in update_jobs[-(-N_MAPS // maps_per_update):]:
            chunk_sub_update(*job)
        for ch in chunks:
            chunk_end(ch)

    def far(j, carry):
        step(j, "far")
        return carry

    lax.fori_loop(0, qi - 1, far, 0)

    @pl.when(qi >= 1)
    def _():
        step(qi - 1, "near")

    step(qi, "diag")

    @pl.when((bi == pl.num_programs(0) - 1) & (qi == nq - 1))
    def _():
        for slot in range(min(CHUNK_SLOTS, n_chunks)):
            for cp in chunk_copies(n_chunks - 1, slot):
                cp.wait()

    for hh in range(DA_HEADS):
        ot = (acc_da_ref[2 * hh] * (1.0 / l_ref[2 * hh])
              - lam * (acc_da_ref[2 * hh + 1] * (1.0 / l_ref[2 * hh + 1])))
        oa_ref[:, hh * LANES:(hh + 1) * LANES] = (_rms(ot.T, subg_ref[...]) * (1.0 - lam_init)).astype(BF16)
    for pair in range(MLA_HEADS // 2):
        ha, hb = 2 * pair, 2 * pair + 1
        ot = jnp.concatenate([acc_ml_ref[ha] * (1.0 / l_ref[n_da + ha]),
                              acc_ml_ref[hb] * (1.0 / l_ref[n_da + hb])], axis=0)
        ob_ref[:, pair * LANES:(pair + 1) * LANES] = ot.T.astype(BF16)


def _attention(lam_init, rel_bias, da_lambda, subg, qd, qm, kdb, vdt, km, vmt, batch, seq,
               page_table, q8, qlat, qpe, knew, vnew, cnew, pnew, cache_k, cache_v, cache_c, cache_pt):
    t = TILE
    nq = seq // t
    n_seq, n_pages = page_table.shape
    page = cache_c.shape[1]
    kv_rank = cache_c.shape[2]
    npg = CHUNK_PAGES
    assert n_pages % npg == 0 and page >= BIAS_CUTOFF
    n_groups = n_pages // npg
    n_chunks = n_seq * n_groups
    n_steps = batch * (nq * (nq + 1)) // 2
    chunks_per_step = -(-n_chunks // n_steps)
    assert n_groups % chunks_per_step == 0 and n_chunks * (n_steps + 1) < 2 ** 31
    past = n_pages * page

    tiles = lambda a: a.reshape(batch, nq, t, a.shape[-1])
    once = pl.Buffered(1)
    kv_spec = lambda width: pl.BlockSpec((1, nq, t, width), lambda b, i, pt: (b, 0, 0, 0), pipeline_mode=once)
    vt_spec = lambda width: pl.BlockSpec((nq, width, t), lambda b, i, pt: (b, 0, 0), pipeline_mode=once)
    q_spec = lambda width: pl.BlockSpec((t, width), lambda b, i, pt: (b * nq + i, 0))
    whole = lambda a: pl.BlockSpec(a.shape, lambda b, i, pt: (0,) * a.ndim, pipeline_mode=once)
    hbm = pl.BlockSpec(memory_space=pl.ANY)
    sds = jax.ShapeDtypeStruct
    out_shape = [sds((batch * seq, DA_W), BF16), sds((batch * seq, MLA_VW), BF16),
                 sds((n_seq, DA_HEADS, LANES), F32), sds((n_seq, MLA_HEADS, kv_rank), F32)]
    grid_spec = pltpu.PrefetchScalarGridSpec(
        num_scalar_prefetch=1,
        grid=(batch, nq),
        in_specs=[pl.BlockSpec(memory_space=pltpu.SMEM), whole(da_lambda), whole(subg),
                  q_spec(DA_W), q_spec(MLA_W), kv_spec(DA_W), vt_spec(DA_W), kv_spec(MLA_W), vt_spec(MLA_VW),
                  whole(q8), whole(qlat), whole(qpe), whole(knew), whole(vnew), whole(cnew), whole(pnew),
                  hbm, hbm, hbm, hbm],
        out_specs=[q_spec(DA_W), q_spec(MLA_VW),
                   pl.BlockSpec(out_shape[2].shape, lambda b, i, pt: (0, 0, 0)),
                   pl.BlockSpec(out_shape[3].shape, lambda b, i, pt: (0, 0, 0))],
        scratch_shapes=[pltpu.VMEM((2 * DA_HEADS + 1, t, t), F32), pltpu.VMEM((2 * DA_HEADS, t, LANES), BF16),
                        pltpu.VMEM((N_MAPS, 1, t), F32), pltpu.VMEM((N_MAPS, 1, t), F32),
                        pltpu.VMEM((2 * DA_HEADS, LANES, t), F32), pltpu.VMEM((MLA_HEADS, MLA_VD, t), F32),
                        pltpu.VMEM((CHUNK_SLOTS, npg, page * DA_HEADS, LANES), F32),
                        pltpu.VMEM((CHUNK_SLOTS, npg, page * DA_HEADS, LANES), F32),
                        pltpu.VMEM((CHUNK_SLOTS, npg, page, kv_rank), F32),
                        pltpu.VMEM((CHUNK_SLOTS, npg, MLA_ROPE, page), F32),
                        pltpu.SemaphoreType.DMA((CHUNK_SLOTS,)),
                        pltpu.VMEM((ROWS, 1), F32), pltpu.VMEM((ROWS, 1), F32), pltpu.VMEM((ROWS, 2 * LANES), F32),
                        pltpu.VMEM((ROWS, 1), F32), pltpu.VMEM((ROWS, 1), F32), pltpu.VMEM((ROWS, kv_rank), F32),
                        pltpu.VMEM((ROWS, page * DA_HEADS // 2), F32)],
    )
    return pl.pallas_call(
        functools.partial(_attn_kernel, lam_init, past, page, n_groups, n_chunks, n_steps, chunks_per_step),
        grid_spec=grid_spec,
        out_shape=out_shape,
        compiler_params=pltpu.CompilerParams(dimension_semantics=("arbitrary", "arbitrary"),
                                             vmem_limit_bytes=VMEM_LIMIT),
        name="attention",
    )(page_table, rel_bias, da_lambda, subg, qd, qm, tiles(kdb), vdt, tiles(km), vmt,
      q8, qlat, qpe, knew, vnew, cnew, pnew, cache_k, cache_v, cache_c, cache_pt)


def _headproj_kernel(n_heads, in_w, out_w, x_ref, w_ref, o_ref):
    for hh in range(n_heads):
        x = x_ref[:, hh * in_w:(hh + 1) * in_w].astype(BF16)
        o_ref[:, hh * out_w:(hh + 1) * out_w] = _dot(x, w_ref[hh]).astype(o_ref.dtype)


def _headproj(x, w, out_dtype, name):
    n_heads, in_w, out_w = w.shape
    r = x.shape[0]
    return pl.pallas_call(
        functools.partial(_headproj_kernel, n_heads, in_w, out_w),
        out_shape=jax.ShapeDtypeStruct((r, n_heads * out_w), out_dtype),
        compiler_params=pltpu.CompilerParams(vmem_limit_bytes=VMEM_LIMIT),
        name=name,
    )(x, w)


def _post_kernel(d_ff, x_ref, oa_ref, ob_ref, gt_ref, gt1_ref, sh2_ref, sc2_ref, gt2_ref, ng_ref,
                 wa_ref, wb_ref, wo_ref, wgu_ref, wdn_ref, y_ref):
    tm, d = x_ref.shape
    parts = [slice(0, tm // 2), slice(tm // 2, tm)] if tm >= 2 * TILE else [slice(0, tm)]

    def mod(ref, rows):
        return ref[0] if ref.shape[1] == 1 else ref[0, rows, :]

    merged = []
    for rows in parts:
        gates = gt_ref[rows, :]
        merged.append((gates[:, :d].astype(F32) * _dot(oa_ref[rows, :], wa_ref[...])
                       + gates[:, d:].astype(F32) * _dot(ob_ref[rows, :], wb_ref[...])).astype(BF16))
    mixed = [_dot(m, wo_ref[...]) for m in merged]
    x1 = [x_ref[rows, :] + mod(gt1_ref, rows) * _rms(m, ng_ref[1:2]) for rows, m in zip(parts, mixed)]
    h2 = [(_rms(v, ng_ref[2:3]) * (1.0 + mod(sc2_ref, rows)) + mod(sh2_ref, rows)).astype(BF16)
          for rows, v in zip(parts, x1)]
    act = []
    for h in h2:
        gg = _dot(h, wgu_ref[:, :d_ff])
        uu = _dot(h, wgu_ref[:, d_ff:])
        act.append((gg * jax.nn.sigmoid(gg) * uu).astype(BF16))
    down = [_dot(a, wdn_ref[...]) for a in act]
    for rows, v, dn in zip(parts, x1, down):
        y_ref[rows, :] = v + mod(gt2_ref, rows) * _rms(dn, ng_ref[3:4])


def _post(x2d, oa, ob, gates, mods, w, tm, rows_per_mod_block):
    t, d = x2d.shape
    d_ff = w["w_dn"].shape[0]
    rmod = mods[0].shape[1]
    row = lambda width: pl.BlockSpec((tm, width), lambda i: (i, 0))
    mod_spec = pl.BlockSpec((1, rmod, d), lambda i: (i // rows_per_mod_block, 0, 0))
    weights = [w["ng"], w["w_a"], w["w_b"], w["w_o"], w["w_gu"], w["w_dn"]]
    return pl.pallas_call(
        functools.partial(_post_kernel, d_ff),
        grid=(t // tm,),
        in_specs=[row(d), row(oa.shape[1]), row(ob.shape[1]), row(2 * d)] + [mod_spec] * 4
                 + [_const_spec(a.shape) for a in weights],
        out_specs=row(d),
        out_shape=jax.ShapeDtypeStruct((t, d), F32),
        compiler_params=pltpu.CompilerParams(dimension_semantics=("arbitrary",), vmem_limit_bytes=VMEM_LIMIT),
        name="post",
    )(x2d, oa, ob, gates, *mods, *weights)


def _layer_weights(norm_g, w_in, mla_q_norm_g, mla_kv_norm_g, mla_w_uq, mla_w_uk, mla_w_uv, w_branch_a,
                   w_branch_b, w_o, ffn_w_gu, ffn_w_down):
    d = w_in.shape[0]
    q_rank = mla_q_norm_g.shape[0]
    kv_rank = mla_kv_norm_g.shape[0]
    half = ROPE_HALF
    c0 = 3 * DA_W + q_rank + kv_rank
    w_kpe = w_in[:, c0:c0 + MLA_ROPE]
    partner = jnp.concatenate([w_kpe[:, half:], w_kpe[:, :half]], axis=1)
    w_main = jnp.concatenate([w_in[:, :c0], w_kpe, w_kpe, partner, partner], axis=1)
    w_g = w_in[:, c0 + MLA_ROPE:]

    zq = lambda n: jnp.zeros((q_rank, MLA_HEADS, n), F32)
    nope, r1, r2 = mla_w_uq[..., :MLA_NOPE], mla_w_uq[..., MLA_NOPE:MLA_NOPE + half], mla_w_uq[..., MLA_NOPE + half:]
    w_uq = jnp.concatenate([zq(half), r1, r2, zq(half), nope], axis=-1).reshape(q_rank, MLA_W)
    zk = lambda n: jnp.zeros((kv_rank, MLA_HEADS, n), F32)
    w_ukp = jnp.concatenate([zk(LANES - MLA_NOPE), mla_w_uk], axis=-1).reshape(kv_rank, MLA_W)
    w_uvt = mla_w_uv.reshape(kv_rank, MLA_VW).T
    w_lat = jnp.concatenate([jnp.zeros((MLA_HEADS, LANES - MLA_NOPE, kv_rank), F32),
                             jnp.transpose(mla_w_uk, (1, 2, 0))], axis=1)
    w_val = jnp.transpose(mla_w_uv, (1, 0, 2))
    bf = lambda a: a.astype(BF16)
    return dict(
        g0=norm_g[0:1], ng=norm_g, gq=mla_q_norm_g[None], gkv=mla_kv_norm_g[None],
        w_main=bf(w_main), w_g=bf(w_g), w_uq=bf(w_uq), w_ukp=bf(w_ukp), w_uvt=bf(w_uvt),
        w_lat=bf(w_lat), w_val=bf(w_val),
        w_a=bf(w_branch_a), w_b=bf(w_branch_b), w_o=bf(w_o), w_gu=bf(ffn_w_gu), w_dn=bf(ffn_w_down))


def _rope_tables(pos):
    freqs = ROPE_THETA ** (-jnp.arange(0, MLA_ROPE, 2, dtype=F32) / MLA_ROPE)
    ang = pos.astype(F32)[:, None] * freqs[None, :]
    cos, sin = jnp.cos(ang), jnp.sin(ang)
    n = pos.shape[0]
    one = jnp.ones((n, MLA_NOPE), F32)
    zero = lambda w: jnp.zeros((n, w), F32)
    cosq = jnp.concatenate([zero(ROPE_HALF), cos, cos, zero(ROPE_HALF), one], axis=1) * MLA_SCALE
    sinq = jnp.concatenate([zero(ROPE_HALF), -sin, sin, zero(ROPE_HALF), zero(MLA_NOPE)], axis=1) * MLA_SCALE
    part = jnp.concatenate([-sin, sin], axis=1)
    tabk = jnp.concatenate([cos, cos, cos, cos, part, part], axis=1)
    return cosq, sinq, tabk


def _layer(l, x_prompt, x_sample, cache_k, cache_v, cache_c, cache_p, page_table, c_prompt, c_sample, rel_bias,
           ada_w, ada_b, da_lambda, da_subln_g, w):
    batch, seq, d = x_prompt.shape
    n_seq, dec_seq, _ = x_sample.shape
    assert dec_seq == 1
    lam_init = 0.8 - 0.6 * math.exp(-0.3 * l)
    n_pool, page = cache_k.shape[1:3]
    past = page_table.shape[1] * page
    subg = da_subln_g[None]

    mod = _adaln(jnp.concatenate([c_prompt, c_sample], axis=0), ada_w, ada_b)
    mods_p = [mod[:batch, i * d:(i + 1) * d].reshape(batch, 1, d) for i in range(6)]
    mods_s = [mod[batch:, i * d:(i + 1) * d].reshape(1, n_seq, d) for i in range(6)]

    xp = x_prompt.reshape(batch * seq, d)
    tm = ROW_TILE
    assert seq % tm == 0 and n_seq % SUBLANES == 0
    (qd, kd, vd, kdb, vdt, ckv, kpe, qm, km, vmt, gates) = _inproj(
        xp, mods_p[1], mods_p[0], w, _rope_tables(jnp.arange(seq, dtype=jnp.int32)), tm, seq // tm)
    xs = x_sample.reshape(n_seq, d)
    pos_s = jnp.full((n_seq,), past, jnp.int32)
    (qd_s, kd_s, vd_s, _, _, ckv_s, kpe_s, qm_s, _, _, gates_s) = _inproj(
        xs, mods_s[1], mods_s[0], w, _rope_tables(pos_s), n_seq, 1)
    kv_rank = ckv_s.shape[1]
    qlat = _headproj(qm_s, w["w_lat"], BF16, "latent_query").reshape(n_seq, MLA_HEADS, kv_rank)
    qpe = qm_s.reshape(n_seq, MLA_HEADS, LANES)[:, :, ROPE_HALF:ROPE_HALF + MLA_ROPE]
    pad_rows = lambda a, r: jnp.pad(a, ((0, 0), (0, r - a.shape[1]), (0, 0)))

    oa, ob, oa_s, olat = _attention(
        lam_init, rel_bias, da_lambda, subg, qd, qm, kdb, vdt, km, vmt, batch, seq, page_table,
        pad_rows(qd_s.reshape(n_seq, DA_HEADS, LANES), SUBLANES), pad_rows(qlat, ROWS), pad_rows(qpe, ROWS),
        pad_rows(kd_s, SUBLANES), pad_rows(vd_s, SUBLANES), ckv_s, kpe_s,
        cache_k.reshape(n_pool, page * DA_HEADS, LANES), cache_v.reshape(n_pool, page * DA_HEADS, LANES),
        cache_c.reshape(n_pool, page, kv_rank), jnp.swapaxes(cache_p.reshape(n_pool, page, MLA_ROPE), 1, 2))

    y_p = _post(xp, oa, ob, gates, [mods_p[2], mods_p[3], mods_p[4], mods_p[5]], w, tm, seq // tm)
    ob_s = _headproj(olat.reshape(n_seq, MLA_HEADS * kv_rank), w["w_val"], BF16, "value_up")
    y_s = _post(xs, oa_s.reshape(n_seq, DA_W).astype(BF16), ob_s, gates_s,
                [mods_s[2], mods_s[3], mods_s[4], mods_s[5]], w, n_seq, 1)
    state_p = (kd.reshape(batch, seq, DA_HEADS, 2 * DA_DH), vd.reshape(batch, seq, DA_HEADS, DA_VD),
               ckv.reshape(batch, seq, -1), kpe.reshape(batch, seq, MLA_ROPE))
    state_s = (kd_s.reshape(n_seq, 1, DA_HEADS, 2 * DA_DH), vd_s.reshape(n_seq, 1, DA_HEADS, DA_VD),
               ckv_s.reshape(n_seq, 1, -1), kpe_s.reshape(n_seq, 1, MLA_ROPE))
    return y_p.reshape(batch, seq, d), y_s.reshape(n_seq, 1, d), state_p, state_s


def kernel(x_prompt, x_sample, cache_da_k, cache_da_v, cache_mla_ckv, cache_mla_kpe, page_table, c_prompt, c_sample, rel_bias, ada_w, ada_b, norm_g, w_in, da_lambda, da_subln_g, mla_q_norm_g, mla_kv_norm_g, mla_w_uq, mla_w_uk, mla_w_uv, w_branch_a, w_branch_b, w_o, ffn_w_gu, ffn_w_down):
    depth = ada_w.shape[0]
    y_p, y_s = x_prompt, x_sample
    st_p, st_s = [], []
    for l in range(depth):
        w = _layer_weights(norm_g[l], w_in[l], mla_q_norm_g[l], mla_kv_norm_g[l], mla_w_uq[l], mla_w_uk[l],
                           mla_w_uv[l], w_branch_a[l], w_branch_b[l], w_o[l], ffn_w_gu[l], ffn_w_down[l])
        y_p, y_s, sp, ss = _layer(l, y_p, y_s, cache_da_k[l:l + 1], cache_da_v[l:l + 1], cache_mla_ckv[l:l + 1],
                                  cache_mla_kpe[l:l + 1], page_table, c_prompt, c_sample, rel_bias, ada_w[l],
                                  ada_b[l], da_lambda[l], da_subln_g[l], w)
        st_p.append(sp)
        st_s.append(ss)
    stack = lambda sts, i: jnp.stack([s[i] for s in sts])
    return (y_p, y_s, stack(st_p, 0), stack(st_p, 1), stack(st_p, 2), stack(st_p, 3),
            stack(st_s, 0), stack(st_s, 1), stack(st_s, 2), stack(st_s, 3))
```

```python
import functools
import math

import numpy as np
import jax
import jax.numpy as jnp
from jax import lax
from jax.experimental import pallas as pl
from jax.experimental.pallas import tpu as pltpu

F32 = jnp.float32
BF16 = jnp.bfloat16

DA_HEADS = 4
DA_DH = 64
DA_VD = 2 * DA_DH
MLA_HEADS = 8
MLA_NOPE = 64
MLA_ROPE = 32
MLA_VD = 64
ROPE_HALF = MLA_ROPE // 2
ROPE_THETA = 10000.0
REL_BUCKETS = 32
REL_MAX_DIST = 128
EPS = 1e-6
LOG2E = math.log2(math.e)
DA_SCALE = DA_DH ** -0.5 * LOG2E
MLA_SCALE = (MLA_NOPE + MLA_ROPE) ** -0.5 * LOG2E

LANES = 128
SUBLANES = 8
DA_W = DA_HEADS * LANES
MLA_W = MLA_HEADS * LANES
MLA_VW = MLA_HEADS * MLA_VD
BIAS_CUTOFF = 113
NEG = -1e30
VMEM_LIMIT = 56 * 1024 * 1024

TILE = 256
ROW_TILE = 512
N_MAPS = 2 * DA_HEADS + MLA_HEADS
CHUNK_PAGES = 16
SUB_PAGES = 4
CHUNK_SLOTS = 2
ROWS = 16

NT_DIMS = (((1,), (1,)), ((), ()))


def _rms(x, g):
    return x * lax.rsqrt(jnp.mean(x * x, axis=-1, keepdims=True) + EPS) * g


def _dot(a, b):
    return jnp.dot(a, b, preferred_element_type=F32)


def _dot_nt(a, b):
    return lax.dot_general(a, b, NT_DIMS, preferred_element_type=F32)


def _const_spec(shape):
    nd = len(shape)
    return pl.BlockSpec(shape, lambda *_: (0,) * nd)


def _adaln_kernel(c_ref, w_ref, b_ref, o_ref):
    c = c_ref[...]
    a = (c * jax.nn.sigmoid(c)).astype(BF16)
    o_ref[...] = _dot(a, w_ref[...].astype(BF16)) + b_ref[...]


def _adaln(c_all, ada_w, ada_b):
    r, d = c_all.shape
    n = ada_w.shape[1]
    tn = 768
    return pl.pallas_call(
        _adaln_kernel,
        grid=(n // tn,),
        in_specs=[pl.BlockSpec((r, d), lambda j: (0, 0)),
                  pl.BlockSpec((d, tn), lambda j: (0, j)),
                  pl.BlockSpec((1, tn), lambda j: (0, j))],
        out_specs=pl.BlockSpec((r, tn), lambda j: (0, j)),
        out_shape=jax.ShapeDtypeStruct((r, n), F32),
        compiler_params=pltpu.CompilerParams(dimension_semantics=("arbitrary",), vmem_limit_bytes=VMEM_LIMIT),
        name="adaln",
    )(c_all, ada_w, ada_b.reshape(1, n))


_C_QD, _C_KD, _C_VD = 0, DA_W, 2 * DA_W
_C_CQ = 3 * DA_W


def _inproj_kernel(q_rank, kv_rank,
                   x_ref, sc_ref, sh_ref, g0_ref, wmain_ref, wg_ref, gq_ref, gkv_ref,
                   wuq_ref, wukp_ref, wuvt_ref, cosq_ref, sinq_ref, tabk_ref,
                   qd_ref, kd_ref, vd_ref, kdb_ref, vdt_ref, ckv_ref, kpe_ref, qm_ref, km_ref, vmt_ref, gt_ref):
    c_ckv = _C_CQ + q_rank
    c_kpe = c_ckv + kv_rank
    n_sub, _, sub = vdt_ref.shape
    parts = [slice(u * sub, (u + 1) * sub) for u in range(n_sub)]

    def mod(ref, rows):
        return ref[0] if ref.shape[1] == 1 else ref[0, rows, :]

    stage1 = []
    for rows in parts:
        h = _rms(x_ref[rows, :], g0_ref[...]) * (1.0 + mod(sc_ref, rows)) + mod(sh_ref, rows)
        hb = h.astype(BF16)
        mm = lambda lo, hi, hb=hb: _dot(hb, wmain_ref[:, lo:hi])
        stage1.append(dict(qd=mm(_C_QD, _C_KD), kd=mm(_C_KD, _C_VD), vd=mm(_C_VD, _C_CQ), cq=mm(_C_CQ, c_ckv),
                           ckv=mm(c_ckv, c_kpe), kx=mm(c_kpe, c_kpe + LANES), gates=_dot(hb, wg_ref[...])))

    stage2 = []
    for u, (rows, s1) in enumerate(zip(parts, stage1)):
        qd_ref[rows, :] = (s1["qd"] * DA_SCALE).astype(BF16)
        for hh in range(DA_HEADS):
            cols = slice(hh * LANES, (hh + 1) * LANES)
            kd_ref[rows, hh, :] = s1["kd"][:, cols]
            vd_ref[rows, hh, :] = s1["vd"][:, cols]
        kdb_ref[rows, :] = s1["kd"].astype(BF16)
        vdt_ref[u] = s1["vd"].T.astype(BF16)
        gt_ref[rows, :] = jax.nn.sigmoid(s1["gates"]).astype(BF16)
        cq = _rms(s1["cq"], gq_ref[...]).astype(BF16)
        ckv = _rms(s1["ckv"], gkv_ref[...])
        ckv_ref[rows, :] = ckv
        r = s1["kx"] * tabk_ref[rows, :]
        kr = r + pltpu.roll(r, LANES // 2, 1)
        kpe_ref[0, :, rows] = kr.T[:MLA_ROPE]
        stage2.append(dict(cq=cq, cb=ckv.astype(BF16), kr=kr))

    for u, (rows, s2) in enumerate(zip(parts, stage2)):
        q1 = _dot(s2["cq"], wuq_ref[...])
        q2 = pltpu.roll(q1, ROPE_HALF, 1) + pltpu.roll(q1, MLA_W - ROPE_HALF, 1)
        cosq = cosq_ref[rows, :]
        sinq = sinq_ref[rows, :]
        kr = s2["kr"]
        lane = lax.broadcasted_iota(jnp.int32, kr.shape, 1)
        in_rope = (lane >= ROPE_HALF) & (lane < ROPE_HALF + MLA_ROPE)
        kslot = jnp.where(in_rope, pltpu.roll(kr, ROPE_HALF, 1), 0.0)
        knope = _dot(s2["cb"], wukp_ref[...])
        for hh in range(MLA_HEADS):
            sl = slice(hh * LANES, (hh + 1) * LANES)
            qm_ref[rows, sl] = (q1[:, sl] * cosq + q2[:, sl] * sinq).astype(BF16)
            km_ref[rows, sl] = (knope[:, sl] + kslot).astype(BF16)
        vmt_ref[u] = _dot_nt(wuvt_ref[...], s2["cb"]).astype(BF16)


def _inproj(x2d, sc, sh, w, tabs, tm, rows_per_mod_block):
    t, d = x2d.shape
    q_rank = w["gq"].shape[1]
    kv_rank = w["gkv"].shape[1]
    rmod = sc.shape[1]
    n_tab_blocks = tabs[0].shape[0] // tm
    n_tiles = t // tm
    row = lambda width: pl.BlockSpec((tm, width), lambda i: (i, 0))
    heads = pl.BlockSpec((tm, DA_HEADS, LANES), lambda i: (i, 0, 0))
    sub = min(tm, TILE)
    n_sub = tm // sub
    transposed = lambda width: pl.BlockSpec((n_sub, width, sub), lambda i: (i, 0, 0))
    mod_spec = pl.BlockSpec((1, rmod, d), lambda i: (i // rows_per_mod_block, 0, 0))
    tab_spec = pl.BlockSpec((tm, LANES), lambda i: (i % n_tab_blocks, 0))
    weights = [w["g0"], w["w_main"], w["w_g"], w["gq"], w["gkv"], w["w_uq"], w["w_ukp"], w["w_uvt"]]
    in_specs = ([row(d), mod_spec, mod_spec] + [_const_spec(a.shape) for a in weights] + [tab_spec] * 3)
    sds = jax.ShapeDtypeStruct
    outs = [
        (row(DA_W), sds((t, DA_W), BF16)),
        (heads, sds((t, DA_HEADS, LANES), F32)),
        (heads, sds((t, DA_HEADS, LANES), F32)),
        (row(DA_W), sds((t, DA_W), BF16)),
        (transposed(DA_W), sds((n_tiles * n_sub, DA_W, sub), BF16)),
        (row(kv_rank), sds((t, kv_rank), F32)),
        (pl.BlockSpec((1, MLA_ROPE, tm), lambda i: (i // rows_per_mod_block, 0, i % rows_per_mod_block)),
         sds((n_tiles // rows_per_mod_block, MLA_ROPE, tm * rows_per_mod_block), F32)),
        (row(MLA_W), sds((t, MLA_W), BF16)),
        (row(MLA_W), sds((t, MLA_W), BF16)),
        (transposed(MLA_VW), sds((n_tiles * n_sub, MLA_VW, sub), BF16)),
        (row(2 * d), sds((t, 2 * d), BF16)),
    ]
    return pl.pallas_call(
        functools.partial(_inproj_kernel, q_rank, kv_rank),
        grid=(n_tiles,),
        in_specs=in_specs,
        out_specs=[o[0] for o in outs],
        out_shape=[o[1] for o in outs],
        compiler_params=pltpu.CompilerParams(dimension_semantics=("arbitrary",), vmem_limit_bytes=VMEM_LIMIT),
        name="inproj",
    )(x2d, sc, sh, *weights, *tabs)


def _t5_bucket(dist):
    n = jnp.maximum(dist, 0)
    max_exact = REL_BUCKETS // 2
    nf = jnp.maximum(n, max_exact).astype(F32)
    large = max_exact + (jnp.log(nf / max_exact) / math.log(REL_MAX_DIST / max_exact)
                         * (REL_BUCKETS - max_exact)).astype(jnp.int32)
    large = jnp.minimum(large, REL_BUCKETS - 1)
    return jnp.where(n < max_exact, n, large)


def _shifted_bias(relb_ref, bucket, head):
    last = relb_ref[REL_BUCKETS - 1, head]
    out = jnp.zeros(bucket.shape, F32)
    for b in range(REL_BUCKETS - 1):
        out = jnp.where(bucket == b, (relb_ref[b, head] - last) * LOG2E, out)
    return out


def _lambda_full(lam_ref, lam_init):
    lf = lam_ref[...]
    a = jnp.sum(lf[0:1] * lf[1:2], axis=1, keepdims=True)
    b = jnp.sum(lf[2:3] * lf[3:4], axis=1, keepdims=True)
    return jnp.exp(a) - jnp.exp(b) + lam_init


def _attn_kernel(lam_init, past, page, n_groups, n_chunks, n_steps, chunks_per_step,
                 pt_ref,
                 relb_ref, lam_ref, subg_ref, qd_ref, qm_ref, kd_ref, vdt_ref, km_ref, vmt_ref,
                 q8_ref, qlat_ref, qpe_ref, knew_ref, vnew_ref, cnew_ref, pnew_ref,
                 ck_hbm, cv_hbm, cc_hbm, cp_hbm,
                 oa_ref, ob_ref, oas_ref, olat_ref,
                 bias_ref, qh_ref, m_ref, l_ref, acc_da_ref, acc_ml_ref,
                 kbuf, vbuf, cbuf, pbuf, sem, mda_ref, lda_ref, ada_ref, mml_ref, lml_ref, aml_ref, near_ref):
    t = TILE
    bi = pl.program_id(0)
    qi = pl.program_id(1)
    nq = pl.num_programs(1)
    n_da = 2 * DA_HEADS
    npg = CHUNK_PAGES
    lam = _lambda_full(lam_ref, lam_init)
    rows_pair = page * DA_HEADS // 2
    row1 = lax.broadcasted_iota(jnp.int32, (ROWS, LANES), 0)
    lane1 = lax.broadcasted_iota(jnp.int32, (ROWS, LANES), 1)
    rowc = lax.broadcasted_iota(jnp.int32, (ROWS, rows_pair), 0)
    colc = lax.broadcasted_iota(jnp.int32, (ROWS, rows_pair), 1)
    own = (colc & 1) == lax.shift_right_logical(rowc & (SUBLANES - 1), 1)

    def chunk_copies(c, slot):
        seq = lax.div(c, n_groups)
        first = lax.rem(c, n_groups) * npg
        copies = []
        for p in range(npg):
            pg = pt_ref[seq, first + p]
            copies += [pltpu.make_async_copy(ck_hbm.at[pg], kbuf.at[slot, p], sem.at[slot]),
                       pltpu.make_async_copy(cv_hbm.at[pg], vbuf.at[slot, p], sem.at[slot]),
                       pltpu.make_async_copy(cc_hbm.at[pg], cbuf.at[slot, p], sem.at[slot]),
                       pltpu.make_async_copy(cp_hbm.at[pg], pbuf.at[slot, p], sem.at[slot])]
        return copies

    def start_chunk(c, slot):
        copies = chunk_copies(c, slot)
        per_page = len(copies) // npg
        for i, cp in enumerate(copies):
            cp.start(priority=(i // per_page) % 2)

    def head_bias(bucket):
        out = jnp.zeros(bucket.shape, F32)
        r = lax.broadcasted_iota(jnp.int32, bucket.shape, 0) & (SUBLANES - 1)
        for hh in range(DA_HEADS):
            out = jnp.where(r == hh, _shifted_bias(relb_ref, bucket, hh), out)
        return out

    def softmax_update(s_list, mx_ref, sum_ref, a_ref, values):
        m_old = mx_ref[...]
        m_new = m_old
        for s in s_list:
            m_new = jnp.maximum(m_new, jnp.max(s, axis=1, keepdims=True))
        alpha = jnp.exp2(m_old - m_new)
        l_new = alpha * sum_ref[...]
        acc = alpha * a_ref[...]
        for s, v in zip(s_list, values):
            p = jnp.exp2(s - m_new)
            l_new = l_new + jnp.sum(p, axis=1, keepdims=True)
            acc = acc + _dot(p.astype(BF16), v)
        mx_ref[...] = m_new
        sum_ref[...] = l_new
        a_ref[...] = acc

    def chunk_begin(c):
        slot = lax.rem(c, CHUNK_SLOTS)
        for cp in chunk_copies(c, slot):
            cp.wait()
        seq = lax.div(c, n_groups)
        grp = lax.rem(c, n_groups)
        q8 = q8_ref[seq].astype(F32)
        q16 = jnp.concatenate([q8, q8], axis=0)
        qhalf = jnp.where(lax.shift_right_logical(lane1, 6) == lax.shift_right_logical(row1, 3), q16, 0.0)
        qrows = jnp.concatenate([jnp.where((row1 & 1) == 0, qhalf, 0.0), jnp.where((row1 & 1) == 1, qhalf, 0.0)],
                                axis=1).astype(BF16)
        qlat = qlat_ref[seq]
        qpe = qpe_ref[seq]

        @pl.when(grp == 0)
        def _():
            kn = knew_ref[seq].astype(BF16).astype(F32)
            kn = jnp.concatenate([kn, kn], axis=0)
            mda_ref[...] = (jnp.sum(qhalf.astype(BF16).astype(F32) * kn, axis=1, keepdims=True)
                            + head_bias(jnp.zeros((ROWS, 1), jnp.int32)))
            lda_ref[...] = jnp.ones((ROWS, 1), F32)
            vn = vnew_ref[seq].astype(BF16).astype(F32)
            vn = jnp.concatenate([vn, vn], axis=0)
            ada_ref[...] = jnp.concatenate([vn, vn], axis=1)
            cn = cnew_ref[pl.ds(seq, 1), :].astype(BF16).astype(F32)
            pn = pnew_ref[pl.ds(seq, 1), :].astype(BF16).astype(F32)
            mml_ref[...] = (jnp.sum(qlat.astype(F32) * cn, axis=1, keepdims=True)
                            + jnp.sum(qpe.astype(F32) * pn, axis=1, keepdims=True))
            lml_ref[...] = jnp.ones((ROWS, 1), F32)
            aml_ref[...] = jnp.broadcast_to(cn, aml_ref.shape)

        return dict(c=c, slot=slot, seq=seq, grp=grp, qrows=qrows, qlat=qlat, qpe=qpe)

    def paired(buf, slot, i):
        even = buf[slot, i, pl.ds(0, rows_pair, stride=2), :]
        odd = buf[slot, i, pl.ds(1, rows_pair, stride=2), :]
        return jnp.concatenate([even, odd], axis=1).astype(BF16)

    def chunk_page_scores(ch, i):
        slot = ch["slot"]
        cb = cbuf[slot, i].astype(BF16)
        s_da = jnp.where(own, _dot_nt(ch["qrows"], paired(kbuf, slot, i)), NEG)
        if i == npg - 1:
            near = near_ref[...]
            s_da = s_da + jnp.where(ch["grp"] == n_groups - 1, near, jnp.zeros_like(near))
        ch.setdefault("cb", {})[i] = cb
        ch.setdefault("s_da", {})[i] = s_da
        ch.setdefault("s_ml", {})[i] = _dot_nt(ch["qlat"], cb) + _dot(ch["qpe"], pbuf[slot, i].astype(BF16))

    def chunk_sub_update(ch, k):
        slot = ch["slot"]
        pages = range(k * SUB_PAGES, (k + 1) * SUB_PAGES)
        softmax_update([ch["s_da"][i] for i in pages], mda_ref, lda_ref, ada_ref,
                       [paired(vbuf, slot, i) for i in pages])
        softmax_update([ch["s_ml"][i] for i in pages], mml_ref, lml_ref, aml_ref, [ch["cb"][i] for i in pages])
        if k == npg // SUB_PAGES - 1:
            start_chunk(jnp.minimum(ch["c"] + CHUNK_SLOTS, n_chunks - 1), slot)

    def chunk_end(ch):
        @pl.when(ch["grp"] == n_groups - 1)
        def _():
            accn = ada_ref[...] * (1.0 / lda_ref[...])
            accn = jnp.where((row1 & 1) == 0, accn[:, :LANES], accn[:, LANES:])
            o = accn[:DA_HEADS] - lam * accn[SUBLANES:SUBLANES + DA_HEADS]
            oas_ref[ch["seq"]] = _rms(o, subg_ref[...]) * (1.0 - lam_init)
            olat_ref[ch["seq"]] = (aml_ref[...] * (1.0 / lml_ref[...]))[:MLA_HEADS]

    @pl.when((bi == 0) & (qi == 0))
    def _():
        for c in range(min(CHUNK_SLOTS, n_chunks)):
            start_chunk(c, c)
        last_page_pos = (n_groups * npg - 1) * page + lax.shift_right_logical(colc, 1)
        near_ref[...] = head_bias(_t5_bucket(past - last_page_pos))

    @pl.when((bi == 0) & (qi == 0))
    def _():
        key = lax.broadcasted_iota(jnp.int32, (t, t), 0)
        qry = lax.broadcasted_iota(jnp.int32, (t, t), 1)
        d0 = qry - key
        b0 = _t5_bucket(d0)
        b1 = _t5_bucket(d0 + t)
        for hh in range(DA_HEADS):
            bias_ref[hh] = jnp.where(d0 >= 0, _shifted_bias(relb_ref, b0, hh), NEG)
            bias_ref[DA_HEADS + hh] = _shifted_bias(relb_ref, b1, hh)
        bias_ref[2 * DA_HEADS] = jnp.where(d0 >= 0, 0.0, NEG)

    lane = lax.broadcasted_iota(jnp.int32, (t, LANES), 1)
    for hh in range(DA_HEADS):
        q = qd_ref[:, hh * LANES:(hh + 1) * LANES].astype(F32)
        qh_ref[2 * hh] = jnp.where(lane < DA_DH, q, 0.0).astype(BF16)
        qh_ref[2 * hh + 1] = jnp.where(lane >= DA_DH, q, 0.0).astype(BF16)

    m_ref[...] = jnp.full(m_ref.shape, NEG, F32)
    l_ref[...] = jnp.zeros(l_ref.shape, F32)
    acc_da_ref[...] = jnp.zeros(acc_da_ref.shape, F32)
    acc_ml_ref[...] = jnp.zeros(acc_ml_ref.shape, F32)

    first_step = bi * ((nq * (nq + 1)) // 2) + (qi * (qi + 1)) // 2

    def step(j, kind):
        n = first_step + j
        if chunks_per_step == 1:
            c = lax.div(n * n_chunks, n_steps)
            has_chunks = lax.div((n + 1) * n_chunks, n_steps) > c
            chunk_ids = [c]
        else:
            has_chunks = (n + 1) * chunks_per_step <= n_chunks
            chunk_ids = [n * chunks_per_step + u for u in range(chunks_per_step)]

        @pl.when(has_chunks)
        def _():
            flash_step(j, kind, chunk_ids)

        @pl.when(jnp.logical_not(has_chunks))
        def _():
            flash_step(j, kind, [])

    def flash_step(j, kind, chunk_ids):
        chunks = [chunk_begin(c) for c in chunk_ids]
        page_jobs = [(ch, p) for ch in chunks for p in range(npg)]
        update_jobs = [(ch, k) for ch in chunks for k in range(npg // SUB_PAGES)]

        def deal(jobs, i):
            per_map = -(-len(jobs) // N_MAPS)
            return jobs[i * per_map:(i + 1) * per_map]

        scores = []
        for i in range(N_MAPS):
            if i < n_da:
                hh = i // 2
                k = kd_ref[0, j, :, hh * LANES:(hh + 1) * LANES]
                q = qh_ref[i]
                bias = {"far": None, "near": DA_HEADS + hh, "diag": hh}[kind]
            else:
                hh = i - n_da
                k = km_ref[0, j, :, hh * LANES:(hh + 1) * LANES]
                q = qm_ref[:, hh * LANES:(hh + 1) * LANES]
                bias = {"far": None, "near": None, "diag": 2 * DA_HEADS}[kind]
            s = _dot_nt(k, q)
            if bias is not None:
                s = s + bias_ref[bias]
            scores.append(s)
            for ch, p in deal(page_jobs, i):
                chunk_page_scores(ch, p)
        probs = []
        for i in range(N_MAPS):
            s = scores[i]
            m_old = m_ref[i]
            m_new = jnp.maximum(m_old, jnp.max(s, axis=0, keepdims=True))
            alpha = jnp.exp2(m_old - m_new)
            p = jnp.exp2(s - m_new)
            l_ref[i] = alpha * l_ref[i] + jnp.sum(p, axis=0, keepdims=True)
            m_ref[i] = m_new
            probs.append((alpha, p.astype(BF16)))
        maps_per_update = max(1, N_MAPS // max(1, len(update_jobs)))
        for i in range(N_MAPS):
            if i % maps_per_update == 0 and i // maps_per_update < len(update_jobs):
                chunk_sub_update(*update_jobs[i // maps_per_update])
            alpha, p = probs[i]
            if i < n_da:
                hh = i // 2
                vt = vdt_ref[j, hh * LANES:(hh + 1) * LANES, :]
                acc_ref, ai = acc_da_ref, i
            else:
                hh = i - n_da
                vt = vmt_ref[j, hh * MLA_VD:(hh + 1) * MLA_VD, :]
                acc_ref, ai = acc_ml_ref, hh
            acc_ref[ai] = alpha * acc_ref[ai] + _dot(vt, p)
        for job in update_jobs[-(-N_MAPS // maps_per_update):]:
            chunk_sub_update(*job)
        for ch in chunks:
            chunk_end(ch)

    def far(j, carry):
        step(j, "far")
        return carry

    lax.fori_loop(0, qi - 1, far, 0)

    @pl.when(qi >= 1)
    def _():
        step(qi - 1, "near")

    step(qi, "diag")

    @pl.when((bi == pl.num_programs(0) - 1) & (qi == nq - 1))
    def _():
        for slot in range(min(CHUNK_SLOTS, n_chunks)):
            for cp in chunk_copies(n_chunks - 1, slot):
                cp.wait()

    for hh in range(DA_HEADS):
        ot = (acc_da_ref[2 * hh] * (1.0 / l_ref[2 * hh])
              - lam * (acc_da_ref[2 * hh + 1] * (1.0 / l_ref[2 * hh + 1])))
        oa_ref[:, hh * LANES:(hh + 1) * LANES] = (_rms(ot.T, subg_ref[...]) * (1.0 - lam_init)).astype(BF16)
    for pair in range(MLA_HEADS // 2):
        ha, hb = 2 * pair, 2 * pair + 1
        ot = jnp.concatenate([acc_ml_ref[ha] * (1.0 / l_ref[n_da + ha]),
                              acc_ml_ref[hb] * (1.0 / l_ref[n_da + hb])], axis=0)
        ob_ref[:, pair * LANES:(pair + 1) * LANES] = ot.T.astype(BF16)


def _attention(lam_init, rel_bias, da_lambda, subg, qd, qm, kdb, vdt, km, vmt, batch, seq,
               page_table, q8, qlat, qpe, knew, vnew, cnew, pnew, cache_k, cache_v, cache_c, cache_pt):
    t = TILE
    nq = seq // t
    n_seq, n_pages = page_table.shape
    page = cache_c.shape[1]
    kv_rank = cache_c.shape[2]
    npg = CHUNK_PAGES
    assert n_pages % npg == 0 and page >= BIAS_CUTOFF
    n_groups = n_pages // npg
    n_chunks = n_seq * n_groups
    n_steps = batch * (nq * (nq + 1)) // 2
    chunks_per_step = -(-n_chunks // n_steps)
    assert n_groups % chunks_per_step == 0 and n_chunks * (n_steps + 1) < 2 ** 31
    past = n_pages * page

    tiles = lambda a: a.reshape(batch, nq, t, a.shape[-1])
    once = pl.Buffered(1)
    kv_spec = lambda width: pl.BlockSpec((1, nq, t, width), lambda b, i, pt: (b, 0, 0, 0), pipeline_mode=once)
    vt_spec = lambda width: pl.BlockSpec((nq, width, t), lambda b, i, pt: (b, 0, 0), pipeline_mode=once)
    q_spec = lambda width: pl.BlockSpec((t, width), lambda b, i, pt: (b * nq + i, 0))
    whole = lambda a: pl.BlockSpec(a.shape, lambda b, i, pt: (0,) * a.ndim, pipeline_mode=once)
    hbm = pl.BlockSpec(memory_space=pl.ANY)
    sds = jax.ShapeDtypeStruct
    out_shape = [sds((batch * seq, DA_W), BF16), sds((batch * seq, MLA_VW), BF16),
                 sds((n_seq, DA_HEADS, LANES), F32), sds((n_seq, MLA_HEADS, kv_rank), F32)]
    grid_spec = pltpu.PrefetchScalarGridSpec(
        num_scalar_prefetch=1,
        grid=(batch, nq),
        in_specs=[pl.BlockSpec(memory_space=pltpu.SMEM), whole(da_lambda), whole(subg),
                  q_spec(DA_W), q_spec(MLA_W), kv_spec(DA_W), vt_spec(DA_W), kv_spec(MLA_W), vt_spec(MLA_VW),
                  whole(q8), whole(qlat), whole(qpe), whole(knew), whole(vnew), whole(cnew), whole(pnew),
                  hbm, hbm, hbm, hbm],
        out_specs=[q_spec(DA_W), q_spec(MLA_VW),
                   pl.BlockSpec(out_shape[2].shape, lambda b, i, pt: (0, 0, 0)),
                   pl.BlockSpec(out_shape[3].shape, lambda b, i, pt: (0, 0, 0))],
        scratch_shapes=[pltpu.VMEM((2 * DA_HEADS + 1, t, t), F32), pltpu.VMEM((2 * DA_HEADS, t, LANES), BF16),
                        pltpu.VMEM((N_MAPS, 1, t), F32), pltpu.VMEM((N_MAPS, 1, t), F32),
                        pltpu.VMEM((2 * DA_HEADS, LANES, t), F32), pltpu.VMEM((MLA_HEADS, MLA_VD, t), F32),
                        pltpu.VMEM((CHUNK_SLOTS, npg, page * DA_HEADS, LANES), F32),
                        pltpu.VMEM((CHUNK_SLOTS, npg, page * DA_HEADS, LANES), F32),
                        pltpu.VMEM((CHUNK_SLOTS, npg, page, kv_rank), F32),
                        pltpu.VMEM((CHUNK_SLOTS, npg, MLA_ROPE, page), F32),
                        pltpu.SemaphoreType.DMA((CHUNK_SLOTS,)),
                        pltpu.VMEM((ROWS, 1), F32), pltpu.VMEM((ROWS, 1), F32), pltpu.VMEM((ROWS, 2 * LANES), F32),
                        pltpu.VMEM((ROWS, 1), F32), pltpu.VMEM((ROWS, 1), F32), pltpu.VMEM((ROWS, kv_rank), F32),
                        pltpu.VMEM((ROWS, page * DA_HEADS // 2), F32)],
    )
    return pl.pallas_call(
        functools.partial(_attn_kernel, lam_init, past, page, n_groups, n_chunks, n_steps, chunks_per_step),
        grid_spec=grid_spec,
        out_shape=out_shape,
        compiler_params=pltpu.CompilerParams(dimension_semantics=("arbitrary", "arbitrary"),
                                             vmem_limit_bytes=VMEM_LIMIT),
        name="attention",
    )(page_table, rel_bias, da_lambda, subg, qd, qm, tiles(kdb), vdt, tiles(km), vmt,
      q8, qlat, qpe, knew, vnew, cnew, pnew, cache_k, cache_v, cache_c, cache_pt)


def _headproj_kernel(n_heads, in_w, out_w, x_ref, w_ref, o_ref):
    for hh in range(n_heads):
        x = x_ref[:, hh * in_w:(hh + 1) * in_w].astype(BF16)
        o_ref[:, hh * out_w:(hh + 1) * out_w] = _dot(x, w_ref[hh]).astype(o_ref.dtype)


def _headproj(x, w, out_dtype, name):
    n_heads, in_w, out_w = w.shape
    r = x.shape[0]
    return pl.pallas_call(
        functools.partial(_headproj_kernel, n_heads, in_w, out_w),
        out_shape=jax.ShapeDtypeStruct((r, n_heads * out_w), out_dtype),
        compiler_params=pltpu.CompilerParams(vmem_limit_bytes=VMEM_LIMIT),
        name=name,
    )(x, w)


def _post_kernel(d_ff, x_ref, oa_ref, ob_ref, gt_ref, gt1_ref, sh2_ref, sc2_ref, gt2_ref, ng_ref,
                 wa_ref, wb_ref, wo_ref, wgu_ref, wdn_ref, y_ref):
    tm, d = x_ref.shape
    parts = [slice(0, tm // 2), slice(tm // 2, tm)] if tm >= 2 * TILE else [slice(0, tm)]

    def mod(ref, rows):
        return ref[0] if ref.shape[1] == 1 else ref[0, rows, :]

    merged = []
    for rows in parts:
        gates = gt_ref[rows, :]
        merged.append((gates[:, :d].astype(F32) * _dot(oa_ref[rows, :], wa_ref[...])
                       + gates[:, d:].astype(F32) * _dot(ob_ref[rows, :], wb_ref[...])).astype(BF16))
    mixed = [_dot(m, wo_ref[...]) for m in merged]
    x1 = [x_ref[rows, :] + mod(gt1_ref, rows) * _rms(m, ng_ref[1:2]) for rows, m in zip(parts, mixed)]
    h2 = [(_rms(v, ng_ref[2:3]) * (1.0 + mod(sc2_ref, rows)) + mod(sh2_ref, rows)).astype(BF16)
          for rows, v in zip(parts, x1)]
    act = []
    for h in h2:
        gg = _dot(h, wgu_ref[:, :d_ff])
        uu = _dot(h, wgu_ref[:, d_ff:])
        act.append((gg * jax.nn.sigmoid(gg) * uu).astype(BF16))
    down = [_dot(a, wdn_ref[...]) for a in act]
    for rows, v, dn in zip(parts, x1, down):
        y_ref[rows, :] = v + mod(gt2_ref, rows) * _rms(dn, ng_ref[3:4])


def _post(x2d, oa, ob, gates, mods, w, tm, rows_per_mod_block):
    t, d = x2d.shape
    d_ff = w["w_dn"].shape[0]
    rmod = mods[0].shape[1]
    row = lambda width: pl.BlockSpec((tm, width), lambda i: (i, 0))
    mod_spec = pl.BlockSpec((1, rmod, d), lambda i: (i // rows_per_mod_block, 0, 0))
    weights = [w["ng"], w["w_a"], w["w_b"], w["w_o"], w["w_gu"], w["w_dn"]]
    return pl.pallas_call(
        functools.partial(_post_kernel, d_ff),
        grid=(t // tm,),
        in_specs=[row(d), row(oa.shape[1]), row(ob.shape[1]), row(2 * d)] + [mod_spec] * 4
                 + [_const_spec(a.shape) for a in weights],
        out_specs=row(d),
        out_shape=jax.ShapeDtypeStruct((t, d), F32),
        compiler_params=pltpu.CompilerParams(dimension_semantics=("arbitrary",), vmem_limit_bytes=VMEM_LIMIT),
        name="post",
    )(x2d, oa, ob, gates, *mods, *weights)


def _layer_weights(norm_g, w_in, mla_q_norm_g, mla_kv_norm_g, mla_w_uq, mla_w_uk, mla_w_uv, w_branch_a,
                   w_branch_b, w_o, ffn_w_gu, ffn_w_down):
    d = w_in.shape[0]
    q_rank = mla_q_norm_g.shape[0]
    kv_rank = mla_kv_norm_g.shape[0]
    half = ROPE_HALF
    c0 = 3 * DA_W + q_rank + kv_rank
    w_kpe = w_in[:, c0:c0 + MLA_ROPE]
    partner = jnp.concatenate([w_kpe[:, half:], w_kpe[:, :half]], axis=1)
    w_main = jnp.concatenate([w_in[:, :c0], w_kpe, w_kpe, partner, partner], axis=1)
    w_g = w_in[:, c0 + MLA_ROPE:]

    zq = lambda n: jnp.zeros((q_rank, MLA_HEADS, n), F32)
    nope, r1, r2 = mla_w_uq[..., :MLA_NOPE], mla_w_uq[..., MLA_NOPE:MLA_NOPE + half], mla_w_uq[..., MLA_NOPE + half:]
    w_uq = jnp.concatenate([zq(half), r1, r2, zq(half), nope], axis=-1).reshape(q_rank, MLA_W)
    zk = lambda n: jnp.zeros((kv_rank, MLA_HEADS, n), F32)
    w_ukp = jnp.concatenate([zk(LANES - MLA_NOPE), mla_w_uk], axis=-1).reshape(kv_rank, MLA_W)
    w_uvt = mla_w_uv.reshape(kv_rank, MLA_VW).T
    w_lat = jnp.concatenate([jnp.zeros((MLA_HEADS, LANES - MLA_NOPE, kv_rank), F32),
                             jnp.transpose(mla_w_uk, (1, 2, 0))], axis=1)
    w_val = jnp.transpose(mla_w_uv, (1, 0, 2))
    bf = lambda a: a.astype(BF16)
    return dict(
        g0=norm_g[0:1], ng=norm_g, gq=mla_q_norm_g[None], gkv=mla_kv_norm_g[None],
        w_main=bf(w_main), w_g=bf(w_g), w_uq=bf(w_uq), w_ukp=bf(w_ukp), w_uvt=bf(w_uvt),
        w_lat=bf(w_lat), w_val=bf(w_val),
        w_a=bf(w_branch_a), w_b=bf(w_branch_b), w_o=bf(w_o), w_gu=bf(ffn_w_gu), w_dn=bf(ffn_w_down))


def _rope_tables(pos):
    f32 = np.float32
    freqs = f32(ROPE_THETA) ** (-np.arange(0, MLA_ROPE, 2, dtype=f32) / f32(MLA_ROPE))
    ang = np.asarray(pos).astype(f32)[:, None] * freqs[None, :]
    cos, sin = np.cos(ang).astype(f32), np.sin(ang).astype(f32)
    n = ang.shape[0]
    one = np.ones((n, MLA_NOPE), f32)
    zero = lambda w: np.zeros((n, w), f32)
    cosq = np.concatenate([zero(ROPE_HALF), cos, cos, zero(ROPE_HALF), one], axis=1) * f32(MLA_SCALE)
    sinq = np.concatenate([zero(ROPE_HALF), -sin, sin, zero(ROPE_HALF), zero(MLA_NOPE)], axis=1) * f32(MLA_SCALE)
    part = np.concatenate([-sin, sin], axis=1)
    tabk = np.concatenate([cos, cos, cos, cos, part, part], axis=1)
    return jnp.asarray(cosq), jnp.asarray(sinq), jnp.asarray(tabk)


def _layer(l, x_prompt, x_sample, cache_k, cache_v, cache_c, cache_p, page_table, c_prompt, c_sample, rel_bias,
           ada_w, ada_b, da_lambda, da_subln_g, w):
    batch, seq, d = x_prompt.shape
    n_seq, dec_seq, _ = x_sample.shape
    assert dec_seq == 1
    lam_init = 0.8 - 0.6 * math.exp(-0.3 * l)
    n_pool, page = cache_k.shape[1:3]
    past = page_table.shape[1] * page
    subg = da_subln_g[None]

    mod = _adaln(jnp.concatenate([c_prompt, c_sample], axis=0), ada_w, ada_b)
    mods_p = [mod[:batch, i * d:(i + 1) * d].reshape(batch, 1, d) for i in range(6)]
    mods_s = [mod[batch:, i * d:(i + 1) * d].reshape(1, n_seq, d) for i in range(6)]

    xp = x_prompt.reshape(batch * seq, d)
    tm = ROW_TILE
    assert seq % tm == 0 and n_seq % SUBLANES == 0
    (qd, kd, vd, kdb, vdt, ckv, kpe_t, qm, km, vmt, gates) = _inproj(
        xp, mods_p[1], mods_p[0], w, _rope_tables(np.arange(seq)), tm, seq // tm)
    kpe = jnp.swapaxes(kpe_t, 1, 2)
    xs = x_sample.reshape(n_seq, d)
    (qd_s, kd_s, vd_s, _, _, ckv_s, kpe_st, qm_s, _, _, gates_s) = _inproj(
        xs, mods_s[1], mods_s[0], w, _rope_tables(np.full((n_seq,), past)), n_seq, 1)
    kpe_s = kpe_st[0].T
    kv_rank = ckv_s.shape[1]
    qlat = _headproj(qm_s, w["w_lat"], BF16, "latent_query").reshape(n_seq, MLA_HEADS, kv_rank)
    qpe = qm_s.reshape(n_seq, MLA_HEADS, LANES)[:, :, ROPE_HALF:ROPE_HALF + MLA_ROPE]
    pad_rows = lambda a, r: jnp.pad(a, ((0, 0), (0, r - a.shape[1]), (0, 0)))

    oa, ob, oa_s, olat = _attention(
        lam_init, rel_bias, da_lambda, subg, qd, qm, kdb, vdt, km, vmt, batch, seq, page_table,
        pad_rows(qd_s.reshape(n_seq, DA_HEADS, LANES), SUBLANES), pad_rows(qlat, ROWS), pad_rows(qpe, ROWS),
        pad_rows(kd_s, SUBLANES), pad_rows(vd_s, SUBLANES), ckv_s, kpe_s,
        cache_k.reshape(n_pool, page * DA_HEADS, LANES), cache_v.reshape(n_pool, page * DA_HEADS, LANES),
        cache_c.reshape(n_pool, page, kv_rank), jnp.swapaxes(cache_p.reshape(n_pool, page, MLA_ROPE), 1, 2))

    y_p = _post(xp, oa, ob, gates, [mods_p[2], mods_p[3], mods_p[4], mods_p[5]], w, tm, seq // tm)
    ob_s = _headproj(olat.reshape(n_seq, MLA_HEADS * kv_rank), w["w_val"], BF16, "value_up")
    y_s = _post(xs, oa_s.reshape(n_seq, DA_W).astype(BF16), ob_s, gates_s,
                [mods_s[2], mods_s[3], mods_s[4], mods_s[5]], w, n_seq, 1)
    state_p = (kd.reshape(batch, seq, DA_HEADS, 2 * DA_DH), vd.reshape(batch, seq, DA_HEADS, DA_VD),
               ckv.reshape(batch, seq, -1), kpe.reshape(batch, seq, MLA_ROPE))
    state_s = (kd_s.reshape(n_seq, 1, DA_HEADS, 2 * DA_DH), vd_s.reshape(n_seq, 1, DA_HEADS, DA_VD),
               ckv_s.reshape(n_seq, 1, -1), kpe_s.reshape(n_seq, 1, MLA_ROPE))
    return y_p.reshape(batch, seq, d), y_s.reshape(n_seq, 1, d), state_p, state_s


def kernel(x_prompt, x_sample, cache_da_k, cache_da_v, cache_mla_ckv, cache_mla_kpe, page_table, c_prompt, c_sample, rel_bias, ada_w, ada_b, norm_g, w_in, da_lambda, da_subln_g, mla_q_norm_g, mla_kv_norm_g, mla_w_uq, mla_w_uk, mla_w_uv, w_branch_a, w_branch_b, w_o, ffn_w_gu, ffn_w_down):
    depth = ada_w.shape[0]
    y_p, y_s = x_prompt, x_sample
    st_p, st_s = [], []
    for l in range(depth):
        w = _layer_weights(norm_g[l], w_in[l], mla_q_norm_g[l], mla_kv_norm_g[l], mla_w_uq[l], mla_w_uk[l],
                           mla_w_uv[l], w_branch_a[l], w_branch_b[l], w_o[l], ffn_w_gu[l], ffn_w_down[l])
        y_p, y_s, sp, ss = _layer(l, y_p, y_s, cache_da_k[l:l + 1], cache_da_v[l:l + 1], cache_mla_ckv[l:l + 1],
                                  cache_mla_kpe[l:l + 1], page_table, c_prompt, c_sample, rel_bias, ada_w[l],
                                  ada_b[l], da_lambda[l], da_subln_g[l], w)
        st_p.append(sp)
        st_s.append(ss)
    stack = lambda sts, i: jnp.stack([s[i] for s in sts])
    return (y_p, y_s, stack(st_p, 0), stack(st_p, 1), stack(st_p, 2), stack(st_p, 3),
            stack(st_s, 0), stack(st_s, 1), stack(st_s, 2), stack(st_s, 3))
```
